```python
import math
import jax, jax.numpy as jnp
from jax import lax
import numpy as np

D_MODEL = 1024
BATCH = 16
SEQ = 2048
DEPTH = 2

N_A_LAYERS = DEPTH // 2
N_B_LAYERS = DEPTH - N_A_LAYERS
S5_GROUP = 16
S5_GROUPS = D_MODEL // S5_GROUP
S5_STATE = 64
SB_HEAD_DIM = 64
SB_HEADS = D_MODEL // SB_HEAD_DIM
D_FF = 4 * D_MODEL
Q_BLOCK = 128
EPS = 1e-6
DT_MIN = 1e-3
DT_MAX = 1e-1

kernel_name = "yoco_s5_stickbreaking_hybrid"


def rms_norm(x, g):
    xf = x.astype(jnp.float32)
    y = xf * lax.rsqrt(jnp.mean(xf * xf, axis=-1, keepdims=True) + EPS)
    return (y * g.astype(jnp.float32)).astype(x.dtype)


def modulate(h, shift, scale):
    return h * (1 + scale[:, None, :]) + shift[:, None, :]


def ada_chunks(c, w, b, n):
    m = jnp.einsum('bd,de->be', jax.nn.silu(c), w) + b
    return jnp.split(m, n, axis=-1)


def _ssm_combine(left, right):
    a1r, a1i, b1r, b1i = left
    a2r, a2i, b2r, b2i = right
    ar = a2r * a1r - a2i * a1i
    ai = a2r * a1i + a2i * a1r
    br = a2r * b1r - a2i * b1i + b2r
    bi = a2r * b1i + a2i * b1r + b2i
    return ar, ai, br, bi


def s5_mixer(u, a_re, a_im, log_dt, b_re, b_im, c_re, c_im, d_skip):
    bsz, seq, _ = u.shape
    f32 = jnp.float32
    uf = u.astype(f32).reshape(bsz, seq, S5_GROUPS, S5_GROUP)
    lam_re = a_re.astype(f32)
    lam_im = a_im.astype(f32)
    dt = jnp.exp(log_dt.astype(f32))[:, None]
    mag = jnp.exp(lam_re * dt)
    ab_re = mag * jnp.cos(lam_im * dt)
    ab_im = mag * jnp.sin(lam_im * dt)
    den = lam_re * lam_re + lam_im * lam_im
    nr = ab_re - 1
    ni = ab_im
    f_re = (nr * lam_re + ni * lam_im) / den
    f_im = (ni * lam_re - nr * lam_im) / den
    br = b_re.astype(f32)
    bi = b_im.astype(f32)
    bb_re = f_re[..., None] * br - f_im[..., None] * bi
    bb_im = f_re[..., None] * bi + f_im[..., None] * br
    bu_re = jnp.einsum('bsgh,gph->bsgp', uf, bb_re)
    bu_im = jnp.einsum('bsgh,gph->bsgp', uf, bb_im)
    a_seq_re = jnp.broadcast_to(ab_re, (1, seq) + ab_re.shape)
    a_seq_im = jnp.broadcast_to(ab_im, (1, seq) + ab_im.shape)
    _, _, st_re, st_im = lax.associative_scan(
        _ssm_combine, (a_seq_re, a_seq_im, bu_re, bu_im), axis=1)
    y = (jnp.einsum('bsgp,ghp->bsgh', st_re, c_re.astype(f32))
         - jnp.einsum('bsgp,ghp->bsgh', st_im, c_im.astype(f32)))
    y = y.reshape(bsz, seq, D_MODEL) + d_skip.astype(f32) * u.astype(f32)
    return y.astype(u.dtype)


def stick_breaking_attention(q, k, v):
    seq = q.shape[2]
    scale = 1.0 / math.sqrt(SB_HEAD_DIM)
    outs = []
    for t0 in range(0, seq, Q_BLOCK):
        t1 = t0 + Q_BLOCK
        qb = q[:, :, t0:t1]
        kb = k[:, :, :t1]
        vb = v[:, :, :t1]
        z = jnp.einsum('bhtd,bhsd->bhts', qb, kb).astype(jnp.float32) * scale
        t_idx = jnp.arange(t0, t1)[:, None]
        s_idx = jnp.arange(t1)[None, :]
        strict = s_idx < t_idx
        log_fail = jnp.where(strict, jax.nn.log_sigmoid(-z), 0.0)
        rev = lax.cumsum(log_fail, axis=3, reverse=True)
        after = jnp.concatenate([rev[..., 1:], jnp.zeros_like(rev[..., :1])], axis=-1)
        w = jnp.where(strict, jnp.exp(jax.nn.log_sigmoid(z) + after), 0.0)
        outs.append(jnp.einsum('bhts,bhsd->bhtd', w.astype(v.dtype), vb))
    return jnp.concatenate(outs, axis=2)


def split_heads(t):
    bsz, seq, _ = t.shape
    return t.reshape(bsz, seq, SB_HEADS, SB_HEAD_DIM)


def setup_inputs(seed: int = 0) -> dict:
    key = jax.random.key(seed)
    ks = jax.random.split(key, 32)
    f32 = jnp.float32
    D = D_MODEL
    G, P, H = S5_GROUPS, S5_STATE, S5_GROUP
    nrm = lambda k, shape, std: jax.random.normal(k, shape, f32) * std
    x = jax.random.normal(ks[0], (BATCH, SEQ, D), f32)
    c = jax.random.normal(ks[1], (BATCH, D), f32)
    ada_w = nrm(ks[2], (DEPTH, D, 6 * D), 0.5 * D ** -0.5)
    ada_b = nrm(ks[3], (DEPTH, 6 * D), 0.02)
    mix_norm_g = 1.0 + nrm(ks[4], (DEPTH, D), 0.02)
    mlp_norm_g = 1.0 + nrm(ks[5], (DEPTH, D), 0.02)
    mlp_w1 = nrm(ks[6], (DEPTH, D, D_FF), D ** -0.5)
    mlp_w2 = nrm(ks[7], (DEPTH, D_FF, D), D_FF ** -0.5)
    s5_a_re = -0.5 + nrm(ks[8], (N_A_LAYERS, G, P), 0.01)
    s5_a_im = (jnp.float32(math.pi) * jnp.arange(P, dtype=f32))[None, None, :] + nrm(ks[9], (N_A_LAYERS, G, P), 0.01)
    s5_log_dt = jax.random.uniform(ks[10], (N_A_LAYERS, G), f32, math.log(DT_MIN), math.log(DT_MAX))
    s5_b_re = nrm(ks[11], (N_A_LAYERS, G, P, H), (2 * H) ** -0.5)
    s5_b_im = nrm(ks[12], (N_A_LAYERS, G, P, H), (2 * H) ** -0.5)
    s5_c_re = nrm(ks[13], (N_A_LAYERS, G, H, P), P ** -0.5)
    s5_c_im = nrm(ks[14], (N_A_LAYERS, G, H, P), P ** -0.5)
    s5_d = nrm(ks[15], (N_A_LAYERS, D), 1.0)
    s5_w_glu = nrm(ks[16], (N_A_LAYERS, D, 2 * D), D ** -0.5)
    kv_ada_w = nrm(ks[17], (D, 2 * D), 0.5 * D ** -0.5)
    kv_ada_b = nrm(ks[18], (2 * D,), 0.02)
    kv_norm_g = 1.0 + nrm(ks[19], (D,), 0.02)
    w_kv = nrm(ks[20], (D, 2 * D), D ** -0.5)
    k_norm_g = 1.0 + nrm(ks[21], (SB_HEAD_DIM,), 0.02)
    sb_w_q = nrm(ks[22], (N_B_LAYERS, D, D), D ** -0.5)
    q_norm_g = 1.0 + nrm(ks[23], (N_B_LAYERS, SB_HEAD_DIM), 0.02)
    sb_w_o = nrm(ks[24], (N_B_LAYERS, D, D), D ** -0.5)
    return {"x": x, "c": c, "ada_w": ada_w, "ada_b": ada_b,
            "mix_norm_g": mix_norm_g, "mlp_norm_g": mlp_norm_g,
            "mlp_w1": mlp_w1, "mlp_w2": mlp_w2,
            "s5_a_re": s5_a_re, "s5_a_im": s5_a_im, "s5_log_dt": s5_log_dt,
            "s5_b_re": s5_b_re, "s5_b_im": s5_b_im, "s5_c_re": s5_c_re, "s5_c_im": s5_c_im,
            "s5_d": s5_d, "s5_w_glu": s5_w_glu,
            "kv_ada_w": kv_ada_w, "kv_ada_b": kv_ada_b, "kv_norm_g": kv_norm_g,
            "w_kv": w_kv, "k_norm_g": k_norm_g,
            "sb_w_q": sb_w_q, "q_norm_g": q_norm_g, "sb_w_o": sb_w_o}


def reference(x, c, ada_w, ada_b, mix_norm_g, mlp_norm_g, mlp_w1, mlp_w2,
              s5_a_re, s5_a_im, s5_log_dt, s5_b_re, s5_b_im, s5_c_re, s5_c_im,
              s5_d, s5_w_glu, kv_ada_w, kv_ada_b, kv_norm_g, w_kv, k_norm_g,
              sb_w_q, q_norm_g, sb_w_o):
    bsz, seq, _ = x.shape
    k_sh = None
    v_sh = None
    for i in range(DEPTH):
        sh_a, sc_a, g_a, sh_m, sc_m, g_m = ada_chunks(c, ada_w[i], ada_b[i], 6)
        if i < N_A_LAYERS:
            j = i
            h = modulate(rms_norm(x, mix_norm_g[i]), sh_a, sc_a)
            y = s5_mixer(h, s5_a_re[j], s5_a_im[j], s5_log_dt[j], s5_b_re[j], s5_b_im[j],
                         s5_c_re[j], s5_c_im[j], s5_d[j])
            val, gate = jnp.split(jnp.einsum('bsd,de->bse', jax.nn.gelu(y), s5_w_glu[j]), 2, axis=-1)
            mix = val * jax.nn.sigmoid(gate)
        else:
            j = i - N_A_LAYERS
            if j == 0:
                kv_shift, kv_scale = ada_chunks(c, kv_ada_w, kv_ada_b, 2)
                hkv = modulate(rms_norm(x, kv_norm_g), kv_shift, kv_scale)
                k_flat, v_flat = jnp.split(jnp.einsum('bsd,de->bse', hkv, w_kv), 2, axis=-1)
                k_sh = rms_norm(split_heads(k_flat), k_norm_g).transpose(0, 2, 1, 3)
                v_sh = split_heads(v_flat).transpose(0, 2, 1, 3)
            h = modulate(rms_norm(x, mix_norm_g[i]), sh_a, sc_a)
            q = jnp.einsum('bsd,de->bse', h, sb_w_q[j])
            q = rms_norm(split_heads(q), q_norm_g[j]).transpose(0, 2, 1, 3)
            o = stick_breaking_attention(q, k_sh, v_sh)
            o = o.transpose(0, 2, 1, 3).reshape(bsz, seq, D_MODEL)
            mix = jnp.einsum('bsd,de->bse', o, sb_w_o[j])
        x = x + g_a[:, None, :] * mix
        h = modulate(rms_norm(x, mlp_norm_g[i]), sh_m, sc_m)
        ff = jnp.einsum('bsf,fd->bsd', jnp.square(jax.nn.relu(jnp.einsum('bsd,df->bsf', h, mlp_w1[i]))), mlp_w2[i])
        x = x + g_m[:, None, :] * ff
    return x
```

```python
import functools
import math

import jax
import jax.numpy as jnp
from jax import lax
from jax.experimental import pallas as pl
from jax.experimental.pallas import tpu as pltpu

F32 = jnp.float32
BF16 = jnp.bfloat16

EPS = 1e-6
HEAD_DIM = 64
S5_GROUP = 16
S5_STATE = 64

V7X_LANES = 128
V7X_MXU_DIM = 256
V7X_VMEM_BYTES = 64 * 1024 * 1024
VMEM_LIMIT_BYTES = V7X_VMEM_BYTES - 8 * 1024 * 1024

S5_CH_PER_TILE = V7X_MXU_DIM
S5_GROUPS_PER_TILE = S5_CH_PER_TILE // S5_GROUP
S5_CPLX_PER_TILE = S5_GROUPS_PER_TILE * S5_STATE
S5_COLS_PER_TILE = 2 * S5_CPLX_PER_TILE


def _params(sem, vmem=VMEM_LIMIT_BYTES):
    return pltpu.CompilerParams(dimension_semantics=sem, vmem_limit_bytes=vmem)


def _dot(a, b):
    return jnp.dot(a, b, preferred_element_type=F32)


def _rms(x):
    return x * lax.rsqrt(jnp.mean(x * x, axis=-1, keepdims=True) + EPS)


def _ada_body(c_ref, w_ref, b_ref, o_ref):
    c = c_ref[...]
    s = c * jax.nn.sigmoid(c)
    o_ref[...] = jnp.dot(s, w_ref[...], preferred_element_type=F32,
                         precision=lax.Precision.HIGHEST) + b_ref[...]


def _ada(c, w, b, tn=1024):
    nl, d, e = w.shape
    nb = c.shape[0]
    return pl.pallas_call(
        _ada_body,
        grid=(nl, e // tn),
        in_specs=[pl.BlockSpec((nb, d), lambda l, n: (0, 0)),
                  pl.BlockSpec((None, d, tn), lambda l, n: (l, 0, n)),
                  pl.BlockSpec((None, 1, tn), lambda l, n: (l, 0, n))],
        out_specs=pl.BlockSpec((None, nb, tn), lambda l, n: (l, 0, n)),
        out_shape=jax.ShapeDtypeStruct((nl, nb, e), F32),
        compiler_params=_params(("arbitrary", "arbitrary")),
        name="ada",
    )(c, w, b.reshape(nl, 1, e))


def _premod_body(x_ref, g_ref, sh_ref, sc_ref, o_ref):
    o_ref[...] = (_rms(x_ref[...]) * g_ref[...]) * (1 + sc_ref[...]) + sh_ref[...]


def _premod(x, g, sh, sc, tp=256):
    nb, seq, d = x.shape
    vec = pl.BlockSpec((None, 1, d), lambda b, t: (b, 0, 0))
    return pl.pallas_call(
        _premod_body,
        grid=(nb, seq // tp),
        in_specs=[pl.BlockSpec((None, tp, d), lambda b, t: (b, t, 0)),
                  pl.BlockSpec((1, d), lambda b, t: (0, 0)), vec, vec],
        out_specs=pl.BlockSpec((tp, d), lambda b, t: (t, b)),
        out_shape=jax.ShapeDtypeStruct((seq, nb * d), F32),
        compiler_params=_params(("arbitrary", "arbitrary")),
        name="premod",
    )(x, g.reshape(1, d), sh.reshape(nb, 1, d), sc.reshape(nb, 1, d))


def _s5param_body(lre_ref, lim_ref, dt_ref, bre_ref, bim_ref, are_ref, aim_ref, bbre_ref, bbim_ref):
    lam_re = lre_ref[...]
    lam_im = lim_ref[...]
    dt = jnp.exp(dt_ref[...])
    mag = jnp.exp(lam_re * dt)
    ab_re = mag * jnp.cos(lam_im * dt)
    ab_im = mag * jnp.sin(lam_im * dt)
    den = lam_re * lam_re + lam_im * lam_im
    nr = ab_re - 1
    ni = ab_im
    f_re = (nr * lam_re + ni * lam_im) / den
    f_im = (ni * lam_re - nr * lam_im) / den
    br = bre_ref[...]
    bi = bim_ref[...]
    are_ref[...] = ab_re
    aim_ref[...] = ab_im
    bbre_ref[...] = f_re * br - f_im * bi
    bbim_ref[...] = f_re * bi + f_im * br


def _s5param(a_re, a_im, log_dt, b_re, b_im):
    rows, h = b_re.shape
    col = jax.ShapeDtypeStruct((rows, 1), F32)
    mat = jax.ShapeDtypeStruct((rows, h), F32)
    return pl.pallas_call(_s5param_body, out_shape=(col, col, mat, mat),
                          compiler_params=_params(None), name="s5param")(
        a_re, a_im, log_dt, b_re, b_im)


def _s5_body(u_ref, wb_ref, are_ref, aim_ref, wc_ref, d_ref, gy_ref, bu_ref, sbf_ref, st_ref,
             *, nb, tt, wcol):
    @pl.when(pl.program_id(0) == 0)
    def _():
        st_ref[...] = jnp.zeros_like(st_ref)

    n_tiles = u_ref.shape[1] // S5_CH_PER_TILE
    for kc in range(n_tiles):
        ch = slice(kc * S5_CH_PER_TILE, (kc + 1) * S5_CH_PER_TILE)
        cols = slice(kc * S5_COLS_PER_TILE, (kc + 1) * S5_COLS_PER_TILE)
        bu_ref[:, cols] = _dot(u_ref[:, ch].astype(BF16), wb_ref[kc])

        for cc in range(S5_CPLX_PER_TILE // wcol):
            q0 = kc * S5_CPLX_PER_TILE + cc * wcol
            cre = slice(kc * S5_COLS_PER_TILE + cc * wcol, kc * S5_COLS_PER_TILE + (cc + 1) * wcol)
            cim = slice(cre.start + S5_CPLX_PER_TILE, cre.stop + S5_CPLX_PER_TILE)
            ar = jnp.broadcast_to(are_ref[:, q0:q0 + wcol], (nb, wcol))
            ai = jnp.broadcast_to(aim_ref[:, q0:q0 + wcol], (nb, wcol))
            sr = st_ref[:, cre]
            si = st_ref[:, cim]
            for t in range(tt):
                rows = slice(t * nb, (t + 1) * nb)
                nsr = ar * sr - ai * si + bu_ref[rows, cre]
                nsi = ar * si + ai * sr + bu_ref[rows, cim]
                sr, si = nsr, nsi
                sbf_ref[rows, cre] = sr.astype(BF16)
                sbf_ref[rows, cim] = si.astype(BF16)
            st_ref[:, cre] = sr
            st_ref[:, cim] = si

        y = _dot(sbf_ref[:, cols], wc_ref[kc]) + d_ref[:, ch] * u_ref[:, ch]
        gy_ref[:, ch] = jax.nn.gelu(y).astype(BF16)


def _s5(u2, wb, are, aim, wc, d_skip, nb, tt=16, wcol=512):
    rows, d = u2.shape
    seq = rows // nb
    n_tiles = d // S5_CH_PER_TILE
    n_cols = n_tiles * S5_COLS_PER_TILE
    body = functools.partial(_s5_body, nb=nb, tt=tt, wcol=wcol)
    return pl.pallas_call(
        body,
        grid=(seq // tt,),
        in_specs=[pl.BlockSpec((tt * nb, d), lambda t: (t, 0)),
                  pl.BlockSpec((n_tiles, S5_CH_PER_TILE, S5_COLS_PER_TILE), lambda t: (0, 0, 0)),
                  pl.BlockSpec((1, n_cols // 2), lambda t: (0, 0)),
                  pl.BlockSpec((1, n_cols // 2), lambda t: (0, 0)),
                  pl.BlockSpec((n_tiles, S5_COLS_PER_TILE, S5_CH_PER_TILE), lambda t: (0, 0, 0)),
                  pl.BlockSpec((1, d), lambda t: (0, 0))],
        out_specs=pl.BlockSpec((tt * nb, d), lambda t: (t, 0)),
        out_shape=jax.ShapeDtypeStruct((rows, d), BF16),
        scratch_shapes=[pltpu.VMEM((tt * nb, n_cols), F32),
                        pltpu.VMEM((tt * nb, n_cols), BF16),
                        pltpu.VMEM((nb, n_cols), F32)],
        compiler_params=_params(("arbitrary",)),
        name="s5",
    )(u2, wb, are, aim, wc, d_skip.reshape(1, d))


def _s5_block_diag(bb_re, bb_im, c_re, c_im):
    g, p, h = bb_re.shape
    nt = g // S5_GROUPS_PER_TILE
    eye = jnp.eye(S5_GROUPS_PER_TILE, dtype=F32)

    def emb_b(bb):
        t = bb.reshape(nt, S5_GROUPS_PER_TILE, p, h)
        return jnp.einsum('kgph,gG->kghGp', t, eye).reshape(nt, S5_CH_PER_TILE, S5_CPLX_PER_TILE)

    def emb_c(cm):
        t = cm.reshape(nt, S5_GROUPS_PER_TILE, h, p)
        return jnp.einsum('kghp,gG->kgpGh', t, eye).reshape(nt, S5_CPLX_PER_TILE, S5_CH_PER_TILE)

    wb = jnp.concatenate([emb_b(bb_re), emb_b(bb_im)], axis=2).astype(BF16)
    wc = jnp.concatenate([emb_c(c_re), emb_c(-c_im)], axis=1).astype(BF16)
    return wb, wc


def _post_body(a_ref, x_ref, wmix_ref, ga_ref, gn_ref, shm_ref, scm_ref, gm_ref, w1_ref, w2_ref,
               o_ref, *, glu, fc):
    d = x_ref.shape[-1]
    mp = _dot(a_ref[...], wmix_ref[...])
    if glu:
        mix = mp[:, :d] * jax.nn.sigmoid(mp[:, d:])
    else:
        mix = mp
    x1 = x_ref[...] + ga_ref[...] * mix
    h = ((_rms(x1) * gn_ref[...]) * (1 + scm_ref[...]) + shm_ref[...]).astype(BF16)
    ff = None
    for c in range(w1_ref.shape[1] // fc):
        hh = jnp.square(jnp.maximum(_dot(h, w1_ref[:, c * fc:(c + 1) * fc]), 0)).astype(BF16)
        part = _dot(hh, w2_ref[c * fc:(c + 1) * fc, :])
        ff = part if ff is None else ff + part
    o_ref[...] = x1 + gm_ref[...] * ff


def _post(a, a_time_major, x, wmix, g_a, gn, sh_m, sc_m, g_m, w1, w2, glu, tm=512, fc=1024):
    nb, seq, d = x.shape
    tm = min(tm, seq)
    if a_time_major:
        a_spec = pl.BlockSpec((tm, d), lambda b, t: (t, b))
    else:
        a_spec = pl.BlockSpec((None, tm, d), lambda b, t: (b, t, 0))
    vec = pl.BlockSpec((None, 1, d), lambda b, t: (b, 0, 0))
    const = lambda shape: pl.BlockSpec(shape, lambda b, t: (0,) * len(shape),
                                       pipeline_mode=pl.Buffered(1))
    r3 = lambda v: v.reshape(nb, 1, d)
    return pl.pallas_call(
        functools.partial(_post_body, glu=glu, fc=fc),
        grid=(nb, seq // tm),
        in_specs=[a_spec, pl.BlockSpec((None, tm, d), lambda b, t: (b, t, 0)),
                  const(wmix.shape), vec, const((1, d)), vec, vec, vec,
                  const(w1.shape), const(w2.shape)],
        out_specs=pl.BlockSpec((None, tm, d), lambda b, t: (b, t, 0)),
        out_shape=jax.ShapeDtypeStruct((nb, seq, d), F32),
        compiler_params=_params(("arbitrary", "arbitrary")),
        name="post_glu" if glu else "post_attn",
    )(a, x, wmix, r3(g_a), gn.reshape(1, d), r3(sh_m), r3(sc_m), r3(g_m), w1, w2)


def _qkv_body(x_ref, gq_ref, shq_ref, scq_ref, gkv_ref, shk_ref, sck_ref, wq_ref, wkv_ref,
              hones_ref, qg_ref, kg_ref, q_ref, k_ref, v_ref):
    d = x_ref.shape[-1]
    xn = _rms(x_ref[...])
    hq = ((xn * gq_ref[...]) * (1 + scq_ref[...]) + shq_ref[...]).astype(BF16)
    hkv = ((xn * gkv_ref[...]) * (1 + sck_ref[...]) + shk_ref[...]).astype(BF16)
    q = _dot(hq, wq_ref[...])
    kv = _dot(hkv, wkv_ref[...])
    hones = hones_ref[...]

    def head_norm(t, g):
        sq = t * t
        hi = sq.astype(BF16)
        lo = (sq - hi.astype(F32)).astype(BF16)
        parts = []
        for c in range(d // V7X_MXU_DIM):
            sl = slice(c * V7X_MXU_DIM, (c + 1) * V7X_MXU_DIM)
            parts.append(_dot(hi[:, sl], hones) + _dot(lo[:, sl], hones))
        ms = jnp.concatenate(parts, axis=1) * (1.0 / HEAD_DIM)
        return (t * lax.rsqrt(ms + EPS)) * g

    q_ref[...] = (head_norm(q, qg_ref[...]) * (1.0 / math.sqrt(HEAD_DIM))).astype(BF16)
    k_ref[...] = head_norm(kv[:, :d], kg_ref[...]).astype(BF16)
    v_ref[...] = kv[:, d:].astype(BF16)


def _qkv(x, gq, shq, scq, gkv, shk, sck, wq, wkv, qg, kg, tm=512):
    nb, seq, d = x.shape
    tm = min(tm, seq)
    vec = pl.BlockSpec((None, 1, d), lambda b, t: (b, 0, 0))
    const = lambda shape: pl.BlockSpec(shape, lambda b, t: (0,) * len(shape),
                                       pipeline_mode=pl.Buffered(1))
    tile = pl.BlockSpec((None, tm, d), lambda b, t: (b, t, 0))
    r3 = lambda v: v.reshape(nb, 1, d)
    idx = jnp.arange(V7X_MXU_DIM) // HEAD_DIM
    hones = (idx[:, None] == idx[None, :]).astype(BF16)
    n_heads = d // HEAD_DIM
    out = jax.ShapeDtypeStruct((nb, seq, d), BF16)
    return pl.pallas_call(
        _qkv_body,
        grid=(nb, seq // tm),
        in_specs=[tile, const((1, d)), vec, vec, const((1, d)), vec, vec,
                  const(wq.shape), const(wkv.shape), const(hones.shape), const((1, d)), const((1, d))],
        out_specs=(tile, tile, tile),
        out_shape=(out, out, out),
        compiler_params=_params(("arbitrary", "arbitrary")),
        name="qkv",
    )(x, gq.reshape(1, d), r3(shq), r3(scq), gkv.reshape(1, d), r3(shk), r3(sck), wq, wkv, hones,
      jnp.tile(qg, n_heads).reshape(1, d), jnp.tile(kg, n_heads).reshape(1, d))


def _attn_body(q_ref, k_ref, v_ref, tri_ref, o_ref, q0_ref, q1_ref, v0_ref, v1_ref,
               c0_ref, c1_ref, acc_ref, *, tq):
    seq = q_ref.shape[0]
    nblk = seq // tq
    head0 = lax.broadcasted_iota(jnp.int32, (seq, V7X_LANES), 1) < HEAD_DIM
    q = q_ref[...]
    v = v_ref[...]
    zero = jnp.zeros_like(q)
    q0_ref[...] = jnp.where(head0, q, zero)
    q1_ref[...] = jnp.where(head0, zero, q)
    v0_ref[...] = jnp.where(head0, v, zero)
    v1_ref[...] = jnp.where(head0, zero, v)
    c0_ref[...] = jnp.zeros_like(c0_ref)
    c1_ref[...] = jnp.zeros_like(c1_ref)
    acc_ref[...] = jnp.zeros_like(acc_ref)

    tri = tri_ref[...]
    strict = (lax.broadcasted_iota(jnp.int32, (tq, tq), 1)
              < lax.broadcasted_iota(jnp.int32, (tq, tq), 0))

    def block(j, i, diag):
        ks = pl.ds(pl.multiple_of(j * tq, tq), tq)
        qs = pl.ds(pl.multiple_of(i * tq, tq), tq)
        kj = k_ref[ks, :]
        pv = None
        for qh_ref, vh_ref, ch_ref in ((q0_ref, v0_ref, c0_ref), (q1_ref, v1_ref, c1_ref)):
            z = lax.dot_general(qh_ref[qs, :], kj, (((1,), (1,)), ((), ())),
                                preferred_element_type=F32)
            sp = jnp.maximum(z, 0) + jnp.log(1 + jnp.exp(-jnp.abs(z)))
            if diag:
                sp = jnp.where(strict, sp, 0)
            hi = sp.astype(BF16)
            lo = (sp - hi.astype(F32)).astype(BF16)
            r = _dot(hi, tri) + _dot(lo, tri)
            c = ch_ref[qs, :]
            w = jnp.exp(z - r - jnp.concatenate([c] * (tq // V7X_LANES), axis=1))
            if diag:
                w = jnp.where(strict, w, 0)
            term = _dot(w.astype(BF16), vh_ref[ks, :])
            pv = term if pv is None else pv + term
            ch_ref[qs, :] = c + jnp.broadcast_to(r[:, 0:1], (tq, V7X_LANES))
        acc_ref[qs, :] += pv

    def outer(jj, carry):
        j = nblk - 1 - jj
        block(j, j, True)

        def inner(i, carry2):
            block(j, i, False)
            return carry2

        lax.fori_loop(j + 1, nblk, inner, 0)
        return carry

    lax.fori_loop(0, nblk, outer, 0)
    o_ref[...] = acc_ref[...].astype(BF16)


def _attn(q, k, v, tq=256):
    nb, seq, d = q.shape
    tq = min(tq, seq)
    npair = d // V7X_LANES
    idx = jnp.arange(tq)
    tri = (idx[:, None] >= idx[None, :]).astype(BF16)
    blk = pl.BlockSpec((None, seq, V7X_LANES), lambda b, p: (b, 0, p))
    return pl.pallas_call(
        functools.partial(_attn_body, tq=tq),
        grid=(nb, npair),
        in_specs=[blk, blk, blk, pl.BlockSpec((tq, tq), lambda b, p: (0, 0))],
        out_specs=blk,
        out_shape=jax.ShapeDtypeStruct((nb, seq, d), BF16),
        scratch_shapes=[pltpu.VMEM((seq, V7X_LANES), BF16)] * 4
        + [pltpu.VMEM((seq, V7X_LANES), F32)] * 3,
        compiler_params=_params(("arbitrary", "arbitrary")),
        name="attn",
    )(q, k, v, tri)


def kernel(x, c, ada_w, ada_b, mix_norm_g, mlp_norm_g, mlp_w1, mlp_w2, s5_a_re, s5_a_im, s5_log_dt,
           s5_b_re, s5_b_im, s5_c_re, s5_c_im, s5_d, s5_w_glu, kv_ada_w, kv_ada_b, kv_norm_g, w_kv,
           k_norm_g, sb_w_q, q_norm_g, sb_w_o):
    nb, seq, d = x.shape
    depth = ada_w.shape[0]
    n_a = s5_a_re.shape[0]
    assert d % S5_CH_PER_TILE == 0 and nb % 16 == 0 and seq % 256 == 0

    mods = _ada(c, ada_w, ada_b)
    kvmod = _ada(c, kv_ada_w[None], kv_ada_b[None])[0]
    bf = lambda w: w.astype(BF16)

    q = k = v = None
    for i in range(depth):
        sh_a, sc_a, g_a, sh_m, sc_m, g_m = [mods[i][:, n * d:(n + 1) * d] for n in range(6)]
        if i < n_a:
            g, p = s5_a_re.shape[1:]
            rows = g * p
            dt_rows = jnp.repeat(s5_log_dt[i], p).reshape(rows, 1)
            are, aim, bb_re, bb_im = _s5param(
                s5_a_re[i].reshape(rows, 1), s5_a_im[i].reshape(rows, 1), dt_rows,
                s5_b_re[i].reshape(rows, S5_GROUP), s5_b_im[i].reshape(rows, S5_GROUP))
            wb, wc = _s5_block_diag(bb_re.reshape(g, p, S5_GROUP), bb_im.reshape(g, p, S5_GROUP),
                                    s5_c_re[i], s5_c_im[i])
            u2 = _premod(x, mix_norm_g[i], sh_a, sc_a)
            gy = _s5(u2.reshape(seq * nb, d), wb, are.reshape(1, rows), aim.reshape(1, rows), wc,
                     s5_d[i], nb)
            x = _post(gy.reshape(seq, nb * d), True, x, bf(s5_w_glu[i]), g_a, mlp_norm_g[i],
                      sh_m, sc_m, g_m, bf(mlp_w1[i]), bf(mlp_w2[i]), glu=True)
        else:
            j = i - n_a
            if j == 0:
                k = v = None
            kv_sh, kv_sc = kvmod[:, :d], kvmod[:, d:]
            qn, kn, vn = _qkv(x, mix_norm_g[i], sh_a, sc_a, kv_norm_g, kv_sh, kv_sc,
                              bf(sb_w_q[j]), bf(w_kv), q_norm_g[j], k_norm_g)
            if k is None:
                k, v = kn, vn
            o = _attn(qn, k, v)
            x = _post(o, False, x, bf(sb_w_o[j]), g_a, mlp_norm_g[i], sh_m, sc_m, g_m,
                      bf(mlp_w1[i]), bf(mlp_w2[i]), glu=False)
    return x
```

```python
import functools
import math

import jax
import jax.numpy as jnp
from jax import lax
from jax.experimental import pallas as pl
from jax.experimental.pallas import tpu as pltpu

F32 = jnp.float32
BF16 = jnp.bfloat16

EPS = 1e-6
HEAD_DIM = 64
S5_GROUP = 16
S5_STATE = 64

V7X_LANES = 128
V7X_MXU_DIM = 256
V7X_VMEM_BYTES = 64 * 1024 * 1024
VMEM_LIMIT_BYTES = V7X_VMEM_BYTES - 8 * 1024 * 1024

S5_CH_PER_TILE = V7X_MXU_DIM
S5_GROUPS_PER_TILE = S5_CH_PER_TILE // S5_GROUP
S5_CPLX_PER_TILE = S5_GROUPS_PER_TILE * S5_STATE
S5_COLS_PER_TILE = 2 * S5_CPLX_PER_TILE


def _params(sem, vmem=VMEM_LIMIT_BYTES):
    return pltpu.CompilerParams(dimension_semantics=sem, vmem_limit_bytes=vmem)


def _dot(a, b):
    return jnp.dot(a, b, preferred_element_type=F32)


def _rms(x):
    return x * lax.rsqrt(jnp.mean(x * x, axis=-1, keepdims=True) + EPS)


def _ada_body(c_ref, w_ref, b_ref, o_ref):
    c = c_ref[...]
    s = c * jax.nn.sigmoid(c)
    o_ref[...] = jnp.dot(s, w_ref[...], preferred_element_type=F32,
                         precision=lax.Precision.HIGHEST) + b_ref[...]


def _ada(c, w, b, tn=1024):
    nl, d, e = w.shape
    nb = c.shape[0]
    return pl.pallas_call(
        _ada_body,
        grid=(nl, e // tn),
        in_specs=[pl.BlockSpec((nb, d), lambda l, n: (0, 0)),
                  pl.BlockSpec((None, d, tn), lambda l, n: (l, 0, n)),
                  pl.BlockSpec((None, 1, tn), lambda l, n: (l, 0, n))],
        out_specs=pl.BlockSpec((None, nb, tn), lambda l, n: (l, 0, n)),
        out_shape=jax.ShapeDtypeStruct((nl, nb, e), F32),
        compiler_params=_params(("arbitrary", "arbitrary")),
        name="ada",
    )(c, w, b.reshape(nl, 1, e))


def _premod_body(x_ref, g_ref, sh_ref, sc_ref, o_ref):
    o_ref[...] = (_rms(x_ref[...]) * g_ref[...]) * (1 + sc_ref[...]) + sh_ref[...]


def _premod(x, g, sh, sc, tp=256):
    nb, seq, d = x.shape
    vec = pl.BlockSpec((None, 1, d), lambda b, t: (b, 0, 0))
    return pl.pallas_call(
        _premod_body,
        grid=(nb, seq // tp),
        in_specs=[pl.BlockSpec((None, tp, d), lambda b, t: (b, t, 0)),
                  pl.BlockSpec((1, d), lambda b, t: (0, 0)), vec, vec],
        out_specs=pl.BlockSpec((tp, d), lambda b, t: (t, b)),
        out_shape=jax.ShapeDtypeStruct((seq, nb * d), F32),
        compiler_params=_params(("arbitrary", "arbitrary")),
        name="premod",
    )(x, g.reshape(1, d), sh.reshape(nb, 1, d), sc.reshape(nb, 1, d))


def _s5param_body(lre_ref, lim_ref, dt_ref, bre_ref, bim_ref, are_ref, aim_ref, bbre_ref, bbim_ref):
    lam_re = lre_ref[...]
    lam_im = lim_ref[...]
    dt = jnp.exp(dt_ref[...])
    mag = jnp.exp(lam_re * dt)
    ab_re = mag * jnp.cos(lam_im * dt)
    ab_im = mag * jnp.sin(lam_im * dt)
    den = lam_re * lam_re + lam_im * lam_im
    nr = ab_re - 1
    ni = ab_im
    f_re = (nr * lam_re + ni * lam_im) / den
    f_im = (ni * lam_re - nr * lam_im) / den
    br = bre_ref[...]
    bi = bim_ref[...]
    are_ref[...] = ab_re
    aim_ref[...] = ab_im
    bbre_ref[...] = f_re * br - f_im * bi
    bbim_ref[...] = f_re * bi + f_im * br


def _s5param(a_re, a_im, log_dt, b_re, b_im):
    rows, h = b_re.shape
    col = jax.ShapeDtypeStruct((rows, 1), F32)
    mat = jax.ShapeDtypeStruct((rows, h), F32)
    return pl.pallas_call(_s5param_body, out_shape=(col, col, mat, mat),
                          compiler_params=_params(None), name="s5param")(
        a_re, a_im, log_dt, b_re, b_im)


def _s5_body(u_ref, wb_ref, are_ref, aim_ref, wc_ref, d_ref, gy_ref, bu_ref, sbf_ref, st_ref,
             *, nb, tt, wcol):
    @pl.when(pl.program_id(0) == 0)
    def _():
        st_ref[...] = jnp.zeros_like(st_ref)

    n_tiles = u_ref.shape[1] // S5_CH_PER_TILE
    for kc in range(n_tiles):
        ch = slice(kc * S5_CH_PER_TILE, (kc + 1) * S5_CH_PER_TILE)
        cols = slice(kc * S5_COLS_PER_TILE, (kc + 1) * S5_COLS_PER_TILE)
        bu_ref[:, cols] = _dot(u_ref[:, ch].astype(BF16), wb_ref[kc])

        for cc in range(S5_CPLX_PER_TILE // wcol):
            q0 = kc * S5_CPLX_PER_TILE + cc * wcol
            cre = slice(kc * S5_COLS_PER_TILE + cc * wcol, kc * S5_COLS_PER_TILE + (cc + 1) * wcol)
            cim = slice(cre.start + S5_CPLX_PER_TILE, cre.stop + S5_CPLX_PER_TILE)
            ar = jnp.broadcast_to(are_ref[:, q0:q0 + wcol], (nb, wcol))
            ai = jnp.broadcast_to(aim_ref[:, q0:q0 + wcol], (nb, wcol))
            sr = st_ref[:, cre]
            si = st_ref[:, cim]
            for t in range(tt):
                rows = slice(t * nb, (t + 1) * nb)
                nsr = ar * sr - ai * si + bu_ref[rows, cre]
                nsi = ar * si + ai * sr + bu_ref[rows, cim]
                sr, si = nsr, nsi
                sbf_ref[rows, cre] = sr.astype(BF16)
                sbf_ref[rows, cim] = si.astype(BF16)
            st_ref[:, cre] = sr
            st_ref[:, cim] = si

        y = _dot(sbf_ref[:, cols], wc_ref[kc]) + d_ref[:, ch] * u_ref[:, ch]
        gy_ref[:, ch] = jax.nn.gelu(y).astype(BF16)


def _s5(u2, wb, are, aim, wc, d_skip, nb, tt=16, wcol=512):
    rows, d = u2.shape
    seq = rows // nb
    n_tiles = d // S5_CH_PER_TILE
    n_cols = n_tiles * S5_COLS_PER_TILE
    body = functools.partial(_s5_body, nb=nb, tt=tt, wcol=wcol)
    return pl.pallas_call(
        body,
        grid=(seq // tt,),
        in_specs=[pl.BlockSpec((tt * nb, d), lambda t: (t, 0)),
                  pl.BlockSpec((n_tiles, S5_CH_PER_TILE, S5_COLS_PER_TILE), lambda t: (0, 0, 0)),
                  pl.BlockSpec((1, n_cols // 2), lambda t: (0, 0)),
                  pl.BlockSpec((1, n_cols // 2), lambda t: (0, 0)),
                  pl.BlockSpec((n_tiles, S5_COLS_PER_TILE, S5_CH_PER_TILE), lambda t: (0, 0, 0)),
                  pl.BlockSpec((1, d), lambda t: (0, 0))],
        out_specs=pl.BlockSpec((tt * nb, d), lambda t: (t, 0)),
        out_shape=jax.ShapeDtypeStruct((rows, d), BF16),
        scratch_shapes=[pltpu.VMEM((tt * nb, n_cols), F32),
                        pltpu.VMEM((tt * nb, n_cols), BF16),
                        pltpu.VMEM((nb, n_cols), F32)],
        compiler_params=_params(("arbitrary",)),
        name="s5",
    )(u2, wb, are, aim, wc, d_skip.reshape(1, d))


def _s5_block_diag(bb_re, bb_im, c_re, c_im):
    g, p, h = bb_re.shape
    nt = g // S5_GROUPS_PER_TILE
    eye = jnp.eye(S5_GROUPS_PER_TILE, dtype=F32)

    def emb_b(bb):
        t = bb.reshape(nt, S5_GROUPS_PER_TILE, p, h)
        return jnp.einsum('kgph,gG->kghGp', t, eye).reshape(nt, S5_CH_PER_TILE, S5_CPLX_PER_TILE)

    def emb_c(cm):
        t = cm.reshape(nt, S5_GROUPS_PER_TILE, h, p)
        return jnp.einsum('kghp,gG->kgpGh', t, eye).reshape(nt, S5_CPLX_PER_TILE, S5_CH_PER_TILE)

    wb = jnp.concatenate([emb_b(bb_re), emb_b(bb_im)], axis=2).astype(BF16)
    wc = jnp.concatenate([emb_c(c_re), emb_c(-c_im)], axis=1).astype(BF16)
    return wb, wc


def _post_body(a_ref, x_ref, wmix_ref, ga_ref, gn_ref, shm_ref, scm_ref, gm_ref, w1_ref, w2_ref,
               o_ref, *, glu, fc):
    d = x_ref.shape[-1]
    mp = _dot(a_ref[...], wmix_ref[...])
    if glu:
        mix = mp[:, :d] * jax.nn.sigmoid(mp[:, d:])
    else:
        mix = mp
    x1 = x_ref[...] + ga_ref[...] * mix
    h = ((_rms(x1) * gn_ref[...]) * (1 + scm_ref[...]) + shm_ref[...]).astype(BF16)
    ff = None
    for c in range(w1_ref.shape[1] // fc):
        hh = jnp.square(jnp.maximum(_dot(h, w1_ref[:, c * fc:(c + 1) * fc]), 0)).astype(BF16)
        part = _dot(hh, w2_ref[c * fc:(c + 1) * fc, :])
        ff = part if ff is None else ff + part
    o_ref[...] = x1 + gm_ref[...] * ff


def _post(a, a_time_major, x, wmix, g_a, gn, sh_m, sc_m, g_m, w1, w2, glu, tm=512, fc=1024):
    nb, seq, d = x.shape
    tm = min(tm, seq)
    if a_time_major:
        a_spec = pl.BlockSpec((tm, d), lambda b, t: (t, b))
    else:
        a_spec = pl.BlockSpec((None, tm, d), lambda b, t: (b, t, 0))
    vec = pl.BlockSpec((None, 1, d), lambda b, t: (b, 0, 0))
    const = lambda shape: pl.BlockSpec(shape, lambda b, t: (0,) * len(shape),
                                       pipeline_mode=pl.Buffered(1))
    r3 = lambda v: v.reshape(nb, 1, d)
    return pl.pallas_call(
        functools.partial(_post_body, glu=glu, fc=fc),
        grid=(nb, seq // tm),
        in_specs=[a_spec, pl.BlockSpec((None, tm, d), lambda b, t: (b, t, 0)),
                  const(wmix.shape), vec, const((1, d)), vec, vec, vec,
                  const(w1.shape), const(w2.shape)],
        out_specs=pl.BlockSpec((None, tm, d), lambda b, t: (b, t, 0)),
        out_shape=jax.ShapeDtypeStruct((nb, seq, d), F32),
        compiler_params=_params(("arbitrary", "arbitrary")),
        name="post_glu" if glu else "post_attn",
    )(a, x, wmix, r3(g_a), gn.reshape(1, d), r3(sh_m), r3(sc_m), r3(g_m), w1, w2)


def _qkv_body(x_ref, gq_ref, shq_ref, scq_ref, gkv_ref, shk_ref, sck_ref, wq_ref, wkv_ref,
              hones_ref, qg_ref, kg_ref, q_ref, k_ref, v_ref):
    d = x_ref.shape[-1]
    xn = _rms(x_ref[...])
    hq = ((xn * gq_ref[...]) * (1 + scq_ref[...]) + shq_ref[...]).astype(BF16)
    hkv = ((xn * gkv_ref[...]) * (1 + sck_ref[...]) + shk_ref[...]).astype(BF16)
    q = _dot(hq, wq_ref[...])
    kv = _dot(hkv, wkv_ref[...])
    hones = hones_ref[...]

    def head_norm(t, g):
        sq = t * t
        hi = sq.astype(BF16)
        lo = (sq - hi.astype(F32)).astype(BF16)
        parts = []
        for c in range(d // V7X_MXU_DIM):
            sl = slice(c * V7X_MXU_DIM, (c + 1) * V7X_MXU_DIM)
            parts.append(_dot(hi[:, sl], hones) + _dot(lo[:, sl], hones))
        ms = jnp.concatenate(parts, axis=1) * (1.0 / HEAD_DIM)
        return (t * lax.rsqrt(ms + EPS)) * g

    q_ref[...] = (head_norm(q, qg_ref[...]) * (1.0 / math.sqrt(HEAD_DIM))).astype(BF16)
    k_ref[...] = head_norm(kv[:, :d], kg_ref[...]).astype(BF16)
    v_ref[...] = kv[:, d:].astype(BF16)


def _qkv(x, gq, shq, scq, gkv, shk, sck, wq, wkv, qg, kg, tm=512):
    nb, seq, d = x.shape
    tm = min(tm, seq)
    vec = pl.BlockSpec((None, 1, d), lambda b, t: (b, 0, 0))
    const = lambda shape: pl.BlockSpec(shape, lambda b, t: (0,) * len(shape),
                                       pipeline_mode=pl.Buffered(1))
    tile = pl.BlockSpec((None, tm, d), lambda b, t: (b, t, 0))
    r3 = lambda v: v.reshape(nb, 1, d)
    idx = jnp.arange(V7X_MXU_DIM) // HEAD_DIM
    hones = (idx[:, None] == idx[None, :]).astype(BF16)
    n_heads = d // HEAD_DIM
    out = jax.ShapeDtypeStruct((nb, seq, d), BF16)
    return pl.pallas_call(
        _qkv_body,
        grid=(nb, seq // tm),
        in_specs=[tile, const((1, d)), vec, vec, const((1, d)), vec, vec,
                  const(wq.shape), const(wkv.shape), const(hones.shape), const((1, d)), const((1, d))],
        out_specs=(tile, tile, tile),
        out_shape=(out, out, out),
        compiler_params=_params(("arbitrary", "arbitrary")),
        name="qkv",
    )(x, gq.reshape(1, d), r3(shq), r3(scq), gkv.reshape(1, d), r3(shk), r3(sck), wq, wkv, hones,
      jnp.tile(qg, n_heads).reshape(1, d), jnp.tile(kg, n_heads).reshape(1, d))


ATTN_DONE_MASS = 104.0


def _attn_body(jt_ref, it_ref, q_ref, k_ref, v_ref, tri_ref, o_ref, qm_ref, vm_ref, c_ref, acc_ref,
               flag_ref, *, tq, npp):
    seq = q_ref.shape[0]
    nblk = seq // tq
    head0 = lax.broadcasted_iota(jnp.int32, (tq, V7X_LANES), 1) < HEAD_DIM
    zero = jnp.zeros((tq, V7X_LANES), BF16)
    for pp in range(npp):
        lanes = slice(pp * V7X_LANES, (pp + 1) * V7X_LANES)
        for blk_i in range(nblk):
            rows = slice(blk_i * tq, (blk_i + 1) * tq)
            q = q_ref[rows, lanes]
            v = v_ref[rows, lanes]
            qm_ref[pp, blk_i, 0] = jnp.where(head0, q, zero)
            qm_ref[pp, blk_i, 1] = jnp.where(head0, zero, q)
            vm_ref[pp, blk_i, 0] = jnp.where(head0, v, zero)
            vm_ref[pp, blk_i, 1] = jnp.where(head0, zero, v)
    c_ref[...] = jnp.zeros_like(c_ref)
    acc_ref[...] = jnp.zeros_like(acc_ref)

    tri = tri_ref[...]
    strict = (lax.broadcasted_iota(jnp.int32, (2 * tq, tq), 1)
              < lax.broadcasted_iota(jnp.int32, (2 * tq, tq), 0) % tq)

    def block(j, i, diag):
        ks = pl.ds(pl.multiple_of(j * tq, tq), tq)
        cmin = None
        for pp in range(npp):
            kj = k_ref[ks, pp * V7X_LANES:(pp + 1) * V7X_LANES]
            qi = qm_ref[pp, i].reshape(2 * tq, V7X_LANES)
            z = lax.dot_general(qi, kj, (((1,), (1,)), ((), ())), preferred_element_type=F32)
            sp = jnp.maximum(jnp.log(1 + jnp.exp(jnp.minimum(z, 80.0))), z)
            if diag:
                sp = jnp.where(strict, sp, 0)
            hi = sp.astype(BF16)
            lo = (sp - hi.astype(F32)).astype(BF16)
            r = _dot(hi, tri) + _dot(lo, tri)
            c = c_ref[pp, i]
            w = jnp.exp(z - r - jnp.concatenate([c] * (tq // V7X_LANES), axis=1))
            if diag:
                w = jnp.where(strict, w, 0)
            w = w.astype(BF16)
            pv = _dot(jnp.concatenate([w[:tq], w[tq:]], axis=1),
                      vm_ref[pp, j].reshape(2 * tq, V7X_LANES))
            acc_ref[pp, i] += pv
            c_new = c + jnp.broadcast_to(r[:, 0:1], (2 * tq, V7X_LANES))
            c_ref[pp, i] = c_new
            cmin = c_new if cmin is None else jnp.minimum(cmin, c_new)
        return jnp.min(cmin)

    def diag_step(i, carry):
        block(i, i, True)
        flag_ref[i] = 1
        return carry

    def off_step(n, carry):
        j = jt_ref[n]
        i = it_ref[n]

        @pl.when(flag_ref[i] != 0)
        def _():
            cmin = block(j, i, False)
            flag_ref[i] = (cmin <= ATTN_DONE_MASS).astype(jnp.int32)

        return carry

    lax.fori_loop(0, nblk, diag_step, 0)
    n_off = nblk * (nblk - 1) // 2
    if n_off:
        lax.fori_loop(0, n_off, off_step, 0)
    for pp in range(npp):
        o_ref[:, pp * V7X_LANES:(pp + 1) * V7X_LANES] = (
            acc_ref[pp].reshape(seq, V7X_LANES).astype(BF16))


def _attn(q, k, v, tq=256, npp=2):
    nb, seq, d = q.shape
    tq = min(tq, seq)
    nblk = seq // tq
    lanes = npp * V7X_LANES
    idx = jnp.arange(tq)
    tri = (idx[:, None] >= idx[None, :]).astype(BF16)
    pairs = [(j, i) for j in reversed(range(nblk)) for i in range(j + 1, nblk)] or [(0, 0)]
    jt = jnp.asarray([p[0] for p in pairs], jnp.int32)
    it = jnp.asarray([p[1] for p in pairs], jnp.int32)
    blk = pl.BlockSpec((None, seq, lanes), lambda b, p, jt, it: (b, 0, p))
    return pl.pallas_call(
        functools.partial(_attn_body, tq=tq, npp=npp),
        grid_spec=pltpu.PrefetchScalarGridSpec(
            num_scalar_prefetch=2,
            grid=(nb, d // lanes),
            in_specs=[blk, blk, blk, pl.BlockSpec((tq, tq), lambda b, p, jt, it: (0, 0))],
            out_specs=blk,
            scratch_shapes=[pltpu.VMEM((npp, nblk, 2, tq, V7X_LANES), BF16),
                            pltpu.VMEM((npp, nblk, 2, tq, V7X_LANES), BF16),
                            pltpu.VMEM((npp, nblk, 2 * tq, V7X_LANES), F32),
                            pltpu.VMEM((npp, nblk, tq, V7X_LANES), F32),
                            pltpu.SMEM((nblk,), jnp.int32)]),
        out_shape=jax.ShapeDtypeStruct((nb, seq, d), BF16),
        compiler_params=_params(("arbitrary", "arbitrary")),
        name="attn",
    )(jt, it, q, k, v, tri)


def kernel(x, c, ada_w, ada_b, mix_norm_g, mlp_norm_g, mlp_w1, mlp_w2, s5_a_re, s5_a_im, s5_log_dt,
           s5_b_re, s5_b_im, s5_c_re, s5_c_im, s5_d, s5_w_glu, kv_ada_w, kv_ada_b, kv_norm_g, w_kv,
           k_norm_g, sb_w_q, q_norm_g, sb_w_o):
    nb, seq, d = x.shape
    depth = ada_w.shape[0]
    n_a = s5_a_re.shape[0]
    assert d % S5_CH_PER_TILE == 0 and nb % 16 == 0 and seq % 256 == 0

    mods = _ada(c, ada_w, ada_b)
    kvmod = _ada(c, kv_ada_w[None], kv_ada_b[None])[0]
    bf = lambda w: w.astype(BF16)

    q = k = v = None
    for i in range(depth):
        sh_a, sc_a, g_a, sh_m, sc_m, g_m = [mods[i][:, n * d:(n + 1) * d] for n in range(6)]
        if i < n_a:
            g, p = s5_a_re.shape[1:]
            rows = g * p
            dt_rows = jnp.repeat(s5_log_dt[i], p).reshape(rows, 1)
            are, aim, bb_re, bb_im = _s5param(
                s5_a_re[i].reshape(rows, 1), s5_a_im[i].reshape(rows, 1), dt_rows,
                s5_b_re[i].reshape(rows, S5_GROUP), s5_b_im[i].reshape(rows, S5_GROUP))
            wb, wc = _s5_block_diag(bb_re.reshape(g, p, S5_GROUP), bb_im.reshape(g, p, S5_GROUP),
                                    s5_c_re[i], s5_c_im[i])
            u2 = _premod(x, mix_norm_g[i], sh_a, sc_a)
            gy = _s5(u2.reshape(seq * nb, d), wb, are.reshape(1, rows), aim.reshape(1, rows), wc,
                     s5_d[i], nb)
            x = _post(gy.reshape(seq, nb * d), True, x, bf(s5_w_glu[i]), g_a, mlp_norm_g[i],
                      sh_m, sc_m, g_m, bf(mlp_w1[i]), bf(mlp_w2[i]), glu=True)
        else:
            j = i - n_a
            if j == 0:
                k = v = None
            kv_sh, kv_sc = kvmod[:, :d], kvmod[:, d:]
            qn, kn, vn = _qkv(x, mix_norm_g[i], sh_a, sc_a, kv_norm_g, kv_sh, kv_sc,
                              bf(sb_w_q[j]), bf(w_kv), q_norm_g[j], k_norm_g)
            if k is None:
                k, v = kn, vn
            o = _attn(qn, k, v)
            x = _post(o, False, x, bf(sb_w_o[j]), g_a, mlp_norm_g[i], sh_m, sc_m, g_m,
                      bf(mlp_w1[i]), bf(mlp_w2[i]), glu=False)
    return x
```

```python
import functools
import math

import jax
import jax.numpy as jnp
from jax import lax
from jax.experimental import pallas as pl
from jax.experimental.pallas import tpu as pltpu

F32 = jnp.float32
BF16 = jnp.bfloat16

EPS = 1e-6
HEAD_DIM = 64
S5_GROUP = 16
S5_STATE = 64

V7X_LANES = 128
V7X_MXU_DIM = 256
V7X_VMEM_BYTES = 64 * 1024 * 1024
VMEM_LIMIT_BYTES = V7X_VMEM_BYTES - 8 * 1024 * 1024

S5_CH_PER_TILE = V7X_MXU_DIM
S5_GROUPS_PER_TILE = S5_CH_PER_TILE // S5_GROUP
S5_CPLX_PER_TILE = S5_GROUPS_PER_TILE * S5_STATE
S5_COLS_PER_TILE = 2 * S5_CPLX_PER_TILE


def _params(sem, vmem=VMEM_LIMIT_BYTES):
    return pltpu.CompilerParams(dimension_semantics=sem, vmem_limit_bytes=vmem)


def _dot(a, b):
    return jnp.dot(a, b, preferred_element_type=F32)


def _rms(x):
    return x * lax.rsqrt(jnp.mean(x * x, axis=-1, keepdims=True) + EPS)


def _ada_body(c_ref, w_ref, b_ref, o_ref):
    c = c_ref[...]
    s = c * jax.nn.sigmoid(c)
    o_ref[...] = jnp.dot(s, w_ref[...], preferred_element_type=F32,
                         precision=lax.Precision.HIGHEST) + b_ref[...]


def _ada(c, w, b, tn=1024):
    nl, d, e = w.shape
    nb = c.shape[0]
    return pl.pallas_call(
        _ada_body,
        grid=(nl, e // tn),
        in_specs=[pl.BlockSpec((nb, d), lambda l, n: (0, 0)),
                  pl.BlockSpec((None, d, tn), lambda l, n: (l, 0, n)),
                  pl.BlockSpec((None, 1, tn), lambda l, n: (l, 0, n))],
        out_specs=pl.BlockSpec((None, nb, tn), lambda l, n: (l, 0, n)),
        out_shape=jax.ShapeDtypeStruct((nl, nb, e), F32),
        compiler_params=_params(("arbitrary", "arbitrary")),
        name="ada",
    )(c, w, b.reshape(nl, 1, e))


def _s5param_body(lre_ref, lim_ref, dt_ref, bre_ref, bim_ref, are_ref, aim_ref, bbre_ref, bbim_ref):
    lam_re = lre_ref[...]
    lam_im = lim_ref[...]
    dt = jnp.exp(dt_ref[...])
    mag = jnp.exp(lam_re * dt)
    ab_re = mag * jnp.cos(lam_im * dt)
    ab_im = mag * jnp.sin(lam_im * dt)
    den = lam_re * lam_re + lam_im * lam_im
    nr = ab_re - 1
    ni = ab_im
    f_re = (nr * lam_re + ni * lam_im) / den
    f_im = (ni * lam_re - nr * lam_im) / den
    br = bre_ref[...]
    bi = bim_ref[...]
    are_ref[...] = ab_re
    aim_ref[...] = ab_im
    bbre_ref[...] = f_re * br - f_im * bi
    bbim_ref[...] = f_re * bi + f_im * br


def _s5param(a_re, a_im, log_dt, b_re, b_im):
    rows, h = b_re.shape
    col = jax.ShapeDtypeStruct((rows, 1), F32)
    mat = jax.ShapeDtypeStruct((rows, h), F32)
    return pl.pallas_call(_s5param_body, out_shape=(col, col, mat, mat),
                          compiler_params=_params(None), name="s5param")(
        a_re, a_im, log_dt, b_re, b_im)


def _s5_body(x_ref, g_ref, sh_ref, sc_ref, perm_ref, permt_ref, wb_ref, are_ref, aim_ref, wc_ref,
             d_ref, gy_ref, u_ref, ut_ref, bu_ref, sbf_ref, st_ref, *, nb, tt, nsub, wcol):
    @pl.when(pl.program_id(0) == 0)
    def _():
        st_ref[...] = jnp.zeros_like(st_ref)

    d = x_ref.shape[-1]
    n_tiles = d // S5_CH_PER_TILE
    permt = permt_ref[...]
    rows_sub = nb * tt
    for sub in range(nsub):
        ts = slice(sub * tt, (sub + 1) * tt)
        rs = slice(sub * rows_sub, (sub + 1) * rows_sub)
        u_ref[rs] = ((_rms(x_ref[:, ts, :]) * g_ref[...]) * (1 + sc_ref[...]) + sh_ref[...]
                     ).reshape(rows_sub, d)
        ut_ref[rs] = _dot(perm_ref[...], u_ref[rs].astype(BF16)).astype(BF16)

    n_cc = S5_CPLX_PER_TILE // wcol
    for kc in range(n_tiles):
        ch = slice(kc * S5_CH_PER_TILE, (kc + 1) * S5_CH_PER_TILE)
        bu_ref[kc] = _dot(ut_ref[:, ch], wb_ref[kc])

        for cc in range(n_cc):
            q0 = kc * S5_CPLX_PER_TILE + cc * wcol
            cre = slice(cc * wcol, (cc + 1) * wcol)
            cim = slice(cre.start + S5_CPLX_PER_TILE, cre.stop + S5_CPLX_PER_TILE)
            slab = 2 * (kc * n_cc + cc)
            ar = jnp.broadcast_to(are_ref[:, q0:q0 + wcol], (nb, wcol))
            ai = jnp.broadcast_to(aim_ref[:, q0:q0 + wcol], (nb, wcol))
            sr = st_ref[slab]
            si = st_ref[slab + 1]
            for t in range(nsub * tt):
                rows = slice(t * nb, (t + 1) * nb)
                nsr = ar * sr - ai * si + bu_ref[kc, rows, cre]
                nsi = ar * si + ai * sr + bu_ref[kc, rows, cim]
                sr, si = nsr, nsi
                sbf_ref[kc, rows, cre] = sr.astype(BF16)
                sbf_ref[kc, rows, cim] = si.astype(BF16)
            st_ref[slab] = sr
            st_ref[slab + 1] = si

        y = _dot(sbf_ref[kc], wc_ref[kc])
        y_hi = y.astype(BF16)
        y_lo = (y - y_hi.astype(F32)).astype(BF16)
        for sub in range(nsub):
            ts = slice(sub * tt, (sub + 1) * tt)
            rs = slice(sub * rows_sub, (sub + 1) * rows_sub)
            yb = _dot(permt, y_hi[rs]) + _dot(permt, y_lo[rs]) + d_ref[:, ch] * u_ref[rs, ch]
            gy_ref[:, ts, ch] = jax.nn.gelu(yb).astype(BF16).reshape(nb, tt, S5_CH_PER_TILE)


def _s5(x, g, sh, sc, wb, are, aim, wc, d_skip, tt=16, nsub=2, wcol=512):
    nb, seq, d = x.shape
    rows = nb * tt
    n_tiles = d // S5_CH_PER_TILE
    n_cols = n_tiles * S5_COLS_PER_TILE
    r = jnp.arange(rows)
    perm = ((r[:, None] // nb == r[None, :] % tt) & (r[:, None] % nb == r[None, :] // tt))
    perm = perm.astype(BF16)
    const = lambda shape: pl.BlockSpec(shape, lambda t: (0,) * len(shape),
                                       pipeline_mode=pl.Buffered(1))
    tile = pl.BlockSpec((nb, nsub * tt, d), lambda t: (0, t, 0))
    body = functools.partial(_s5_body, nb=nb, tt=tt, nsub=nsub, wcol=wcol)
    return pl.pallas_call(
        body,
        grid=(seq // (nsub * tt),),
        in_specs=[tile, const((1, d)), const((nb, 1, d)), const((nb, 1, d)),
                  const((rows, rows)), const((rows, rows)),
                  const((n_tiles, S5_CH_PER_TILE, S5_COLS_PER_TILE)),
                  const((1, n_cols // 2)), const((1, n_cols // 2)),
                  const((n_tiles, S5_COLS_PER_TILE, S5_CH_PER_TILE)), const((1, d))],
        out_specs=tile,
        out_shape=jax.ShapeDtypeStruct((nb, seq, d), BF16),
        scratch_shapes=[pltpu.VMEM((nsub * rows, d), F32),
                        pltpu.VMEM((nsub * rows, d), BF16),
                        pltpu.VMEM((n_tiles, nsub * rows, S5_COLS_PER_TILE), F32),
                        pltpu.VMEM((n_tiles, nsub * rows, S5_COLS_PER_TILE), BF16),
                        pltpu.VMEM((n_cols // wcol, nb, wcol), F32)],
        compiler_params=_params(("arbitrary",)),
        name="s5",
    )(x, g.reshape(1, d), sh.reshape(nb, 1, d), sc.reshape(nb, 1, d), perm, perm.T, wb, are, aim,
      wc, d_skip.reshape(1, d))


def _s5_block_diag(bb_re, bb_im, c_re, c_im):
    g, p, h = bb_re.shape
    nt = g // S5_GROUPS_PER_TILE
    eye = jnp.eye(S5_GROUPS_PER_TILE, dtype=F32)

    def emb_b(bb):
        t = bb.reshape(nt, S5_GROUPS_PER_TILE, p, h)
        return jnp.einsum('kgph,gG->kghGp', t, eye).reshape(nt, S5_CH_PER_TILE, S5_CPLX_PER_TILE)

    def emb_c(cm):
        t = cm.reshape(nt, S5_GROUPS_PER_TILE, h, p)
        return jnp.einsum('kghp,gG->kgpGh', t, eye).reshape(nt, S5_CPLX_PER_TILE, S5_CH_PER_TILE)

    wb = jnp.concatenate([emb_b(bb_re), emb_b(bb_im)], axis=2).astype(BF16)
    wc = jnp.concatenate([emb_c(c_re), emb_c(-c_im)], axis=1).astype(BF16)
    return wb, wc


def _post_body(a_ref, x_ref, wmix_ref, ga_ref, gn_ref, shm_ref, scm_ref, gm_ref, w1_ref, w2_ref,
               o_ref, *, glu, fc):
    d = x_ref.shape[-1]
    mp = _dot(a_ref[...], wmix_ref[...])
    if glu:
        mix = mp[:, :d] * jax.nn.sigmoid(mp[:, d:])
    else:
        mix = mp
    x1 = x_ref[...] + ga_ref[...] * mix
    h = ((_rms(x1) * gn_ref[...]) * (1 + scm_ref[...]) + shm_ref[...]).astype(BF16)
    ff = None
    for c in range(w1_ref.shape[1] // fc):
        hh = jnp.square(jnp.maximum(_dot(h, w1_ref[:, c * fc:(c + 1) * fc]), 0)).astype(BF16)
        part = _dot(hh, w2_ref[c * fc:(c + 1) * fc, :])
        ff = part if ff is None else ff + part
    o_ref[...] = x1 + gm_ref[...] * ff


def _post(a, x, wmix, g_a, gn, sh_m, sc_m, g_m, w1, w2, glu, tm=512, fc=1024):
    nb, seq, d = x.shape
    tm = min(tm, seq)
    a_spec = pl.BlockSpec((None, tm, d), lambda b, t: (b, t, 0))
    vec = pl.BlockSpec((None, 1, d), lambda b, t: (b, 0, 0))
    const = lambda shape: pl.BlockSpec(shape, lambda b, t: (0,) * len(shape),
                                       pipeline_mode=pl.Buffered(1))
    r3 = lambda v: v.reshape(nb, 1, d)
    return pl.pallas_call(
        functools.partial(_post_body, glu=glu, fc=fc),
        grid=(nb, seq // tm),
        in_specs=[a_spec, pl.BlockSpec((None, tm, d), lambda b, t: (b, t, 0)),
                  const(wmix.shape), vec, const((1, d)), vec, vec, vec,
                  const(w1.shape), const(w2.shape)],
        out_specs=pl.BlockSpec((None, tm, d), lambda b, t: (b, t, 0)),
        out_shape=jax.ShapeDtypeStruct((nb, seq, d), F32),
        compiler_params=_params(("arbitrary", "arbitrary")),
        name="post_glu" if glu else "post_attn",
    )(a, x, wmix, r3(g_a), gn.reshape(1, d), r3(sh_m), r3(sc_m), r3(g_m), w1, w2)


def _qkv_body(x_ref, gq_ref, shq_ref, scq_ref, gkv_ref, shk_ref, sck_ref, wq_ref, wkv_ref,
              hones_ref, qg_ref, kg_ref, q_ref, k_ref, v_ref):
    d = x_ref.shape[-1]
    xn = _rms(x_ref[...])
    hq = ((xn * gq_ref[...]) * (1 + scq_ref[...]) + shq_ref[...]).astype(BF16)
    hkv = ((xn * gkv_ref[...]) * (1 + sck_ref[...]) + shk_ref[...]).astype(BF16)
    q = _dot(hq, wq_ref[...])
    kv = _dot(hkv, wkv_ref[...])
    hones = hones_ref[...]

    def head_norm(t, g):
        sq = t * t
        hi = sq.astype(BF16)
        lo = (sq - hi.astype(F32)).astype(BF16)
        parts = []
        for c in range(d // V7X_MXU_DIM):
            sl = slice(c * V7X_MXU_DIM, (c + 1) * V7X_MXU_DIM)
            parts.append(_dot(hi[:, sl], hones) + _dot(lo[:, sl], hones))
        ms = jnp.concatenate(parts, axis=1) * (1.0 / HEAD_DIM)
        return (t * lax.rsqrt(ms + EPS)) * g

    q_ref[...] = (head_norm(q, qg_ref[...]) * (1.0 / math.sqrt(HEAD_DIM))).astype(BF16)
    k_ref[...] = head_norm(kv[:, :d], kg_ref[...]).astype(BF16)
    v_ref[...] = kv[:, d:].astype(BF16)


def _qkv(x, gq, shq, scq, gkv, shk, sck, wq, wkv, qg, kg, tm=512):
    nb, seq, d = x.shape
    tm = min(tm, seq)
    vec = pl.BlockSpec((None, 1, d), lambda b, t: (b, 0, 0))
    const = lambda shape: pl.BlockSpec(shape, lambda b, t: (0,) * len(shape),
                                       pipeline_mode=pl.Buffered(1))
    tile = pl.BlockSpec((None, tm, d), lambda b, t: (b, t, 0))
    r3 = lambda v: v.reshape(nb, 1, d)
    idx = jnp.arange(V7X_MXU_DIM) // HEAD_DIM
    hones = (idx[:, None] == idx[None, :]).astype(BF16)
    n_heads = d // HEAD_DIM
    out = jax.ShapeDtypeStruct((nb, seq, d), BF16)
    return pl.pallas_call(
        _qkv_body,
        grid=(nb, seq // tm),
        in_specs=[tile, const((1, d)), vec, vec, const((1, d)), vec, vec,
                  const(wq.shape), const(wkv.shape), const(hones.shape), const((1, d)), const((1, d))],
        out_specs=(tile, tile, tile),
        out_shape=(out, out, out),
        compiler_params=_params(("arbitrary", "arbitrary")),
        name="qkv",
    )(x, gq.reshape(1, d), r3(shq), r3(scq), gkv.reshape(1, d), r3(shk), r3(sck), wq, wkv, hones,
      jnp.tile(qg, n_heads).reshape(1, d), jnp.tile(kg, n_heads).reshape(1, d))


ATTN_DONE_MASS = 104.0


def _attn_body(q_ref, k_ref, v_ref, tri_ref, bias_ref, o_ref, qm_ref, vm_ref, c_ref, acc_ref,
               zb_ref, hi_ref, lo_ref, w_ref, qj_ref, qi_ref, *, tq, npp):
    seq = q_ref.shape[0]
    nblk = seq // tq
    head0 = lax.broadcasted_iota(jnp.int32, (tq, V7X_LANES), 1) < HEAD_DIM
    zero = jnp.zeros((tq, V7X_LANES), BF16)
    for pp in range(npp):
        lanes = slice(pp * V7X_LANES, (pp + 1) * V7X_LANES)
        for blk_i in range(nblk):
            rows = slice(blk_i * tq, (blk_i + 1) * tq)
            q = q_ref[rows, lanes]
            v = v_ref[rows, lanes]
            qm_ref[pp, blk_i, 0] = jnp.where(head0, q, zero)
            qm_ref[pp, blk_i, 1] = jnp.where(head0, zero, q)
            vm_ref[pp, blk_i, 0] = jnp.where(head0, v, zero)
            vm_ref[pp, blk_i, 1] = jnp.where(head0, zero, v)
    c_ref[...] = jnp.zeros_like(c_ref)
    acc_ref[...] = jnp.zeros_like(acc_ref)

    tri = tri_ref[...]
    dummy = nblk

    def score(par, j, i):
        ks = pl.ds(pl.multiple_of(j * tq, tq), tq)
        bias = bias_ref[jnp.asarray(j == i, jnp.int32)]
        for pp in range(npp):
            kj = k_ref[ks, pp * V7X_LANES:(pp + 1) * V7X_LANES]
            qi = qm_ref[pp, i].reshape(2 * tq, V7X_LANES)
            zb = lax.dot_general(qi, kj, (((1,), (1,)), ((), ())),
                                 preferred_element_type=F32) + bias
            sp = jnp.maximum(jnp.log(1 + jnp.exp(jnp.minimum(zb, 80.0))), zb)
            hi = sp.astype(BF16)
            zb_ref[par, pp] = zb
            hi_ref[par, pp] = hi
            lo_ref[par, pp] = (sp - hi.astype(F32)).astype(BF16)

    def mass(par, i, i_dst):
        cmin = None
        for pp in range(npp):
            r = _dot(hi_ref[par, pp], tri) + _dot(lo_ref[par, pp], tri)
            c = c_ref[pp, i]
            w = jnp.exp(zb_ref[par, pp] - r - jnp.concatenate([c] * (tq // V7X_LANES), axis=1))
            w_ref[par, pp] = w.astype(BF16)
            c_new = c + jnp.broadcast_to(r[:, 0:1], (2 * tq, V7X_LANES))
            c_ref[pp, i_dst] = c_new
            cmin = c_new if cmin is None else jnp.minimum(cmin, c_new)
        return jnp.min(cmin)

    def out(par, j, i_dst):
        for pp in range(npp):
            w = w_ref[par, pp]
            pv = _dot(jnp.concatenate([w[:tq], w[tq:]], axis=1),
                      vm_ref[pp, j].reshape(2 * tq, V7X_LANES))
            acc_ref[pp, i_dst] += pv

    for e in range(nblk):
        qj_ref[e] = e
        qi_ref[e] = e
    score(0, 0, 0)
    score(1, 1, 1)
    mass(0, 0, 0)

    def cond(st):
        head, tail, _, _, live_m, _, _, live_o = st
        return (head < tail) | (live_m != 0) | (live_o != 0)

    def step(par, st):
        head, tail, jm, im, live_m, jo, io, live_o = st
        live_s = (head < tail).astype(jnp.int32)
        e = jnp.minimum(head, tail - 1)
        js = qj_ref[e]
        is_ = qi_ref[e]
        score(par, js, is_)
        cmin = mass(1 - par, im, jnp.where(live_m != 0, im, dummy))
        out(par, jo, jnp.where(live_o != 0, io, dummy))
        more = (live_m != 0) & (jm > 0) & (cmin <= ATTN_DONE_MASS)
        qj_ref[tail] = jm - 1
        qi_ref[tail] = im
        return (head + live_s, tail + more.astype(jnp.int32), js, is_, live_s, jm, im, live_m)

    one = jnp.int32(1)
    zero_i = jnp.int32(0)
    lax.while_loop(cond, lambda st: step(1, step(0, st)),
                   (jnp.int32(2), jnp.int32(nblk), one, one, one, zero_i, zero_i, one))
    for pp in range(npp):
        o_ref[:, pp * V7X_LANES:(pp + 1) * V7X_LANES] = (
            acc_ref[pp, :nblk].reshape(seq, V7X_LANES).astype(BF16))


def _attn(q, k, v, tq=256, npp=2):
    nb, seq, d = q.shape
    tq = min(tq, seq)
    nblk = seq // tq
    assert nblk >= 2
    lanes = npp * V7X_LANES
    idx = jnp.arange(tq)
    tri = (idx[:, None] >= idx[None, :]).astype(BF16)
    strict = jnp.tile(idx[None, :] < idx[:, None], (2, 1))
    bias = jnp.stack([jnp.zeros((2 * tq, tq), F32), jnp.where(strict, 0.0, -1e30).astype(F32)])
    n_queue = nblk * (nblk + 1) // 2 + 1
    blk = pl.BlockSpec((None, seq, lanes), lambda b, p: (b, 0, p))
    return pl.pallas_call(
        functools.partial(_attn_body, tq=tq, npp=npp),
        grid=(nb, d // lanes),
        in_specs=[blk, blk, blk, pl.BlockSpec((tq, tq), lambda b, p: (0, 0)),
                  pl.BlockSpec((2, 2 * tq, tq), lambda b, p: (0, 0, 0))],
        out_specs=blk,
        scratch_shapes=[pltpu.VMEM((npp, nblk, 2, tq, V7X_LANES), BF16),
                        pltpu.VMEM((npp, nblk, 2, tq, V7X_LANES), BF16),
                        pltpu.VMEM((npp, nblk + 1, 2 * tq, V7X_LANES), F32),
                        pltpu.VMEM((npp, nblk + 1, tq, V7X_LANES), F32),
                        pltpu.VMEM((2, npp, 2 * tq, tq), F32),
                        pltpu.VMEM((2, npp, 2 * tq, tq), BF16),
                        pltpu.VMEM((2, npp, 2 * tq, tq), BF16),
                        pltpu.VMEM((2, npp, 2 * tq, tq), BF16),
                        pltpu.SMEM((n_queue,), jnp.int32),
                        pltpu.SMEM((n_queue,), jnp.int32)],
        out_shape=jax.ShapeDtypeStruct((nb, seq, d), BF16),
        compiler_params=_params(("arbitrary", "arbitrary")),
        name="attn",
    )(q, k, v, tri, bias)


def kernel(x, c, ada_w, ada_b, mix_norm_g, mlp_norm_g, mlp_w1, mlp_w2, s5_a_re, s5_a_im, s5_log_dt,
           s5_b_re, s5_b_im, s5_c_re, s5_c_im, s5_d, s5_w_glu, kv_ada_w, kv_ada_b, kv_norm_g, w_kv,
           k_norm_g, sb_w_q, q_norm_g, sb_w_o):
    nb, seq, d = x.shape
    depth = ada_w.shape[0]
    n_a = s5_a_re.shape[0]
    assert d % S5_CH_PER_TILE == 0 and nb % 16 == 0 and seq % 256 == 0

    mods = _ada(c, ada_w, ada_b)
    kvmod = _ada(c, kv_ada_w[None], kv_ada_b[None])[0]
    bf = lambda w: w.astype(BF16)

    q = k = v = None
    for i in range(depth):
        sh_a, sc_a, g_a, sh_m, sc_m, g_m = [mods[i][:, n * d:(n + 1) * d] for n in range(6)]
        if i < n_a:
            g, p = s5_a_re.shape[1:]
            rows = g * p
            dt_rows = jnp.repeat(s5_log_dt[i], p).reshape(rows, 1)
            are, aim, bb_re, bb_im = _s5param(
                s5_a_re[i].reshape(rows, 1), s5_a_im[i].reshape(rows, 1), dt_rows,
                s5_b_re[i].reshape(rows, S5_GROUP), s5_b_im[i].reshape(rows, S5_GROUP))
            wb, wc = _s5_block_diag(bb_re.reshape(g, p, S5_GROUP), bb_im.reshape(g, p, S5_GROUP),
                                    s5_c_re[i], s5_c_im[i])
            gy = _s5(x, mix_norm_g[i], sh_a, sc_a, wb, are.reshape(1, rows), aim.reshape(1, rows),
                     wc, s5_d[i])
            x = _post(gy, x, bf(s5_w_glu[i]), g_a, mlp_norm_g[i], sh_m, sc_m, g_m,
                      bf(mlp_w1[i]), bf(mlp_w2[i]), glu=True)
        else:
            j = i - n_a
            if j == 0:
                k = v = None
            kv_sh, kv_sc = kvmod[:, :d], kvmod[:, d:]
            qn, kn, vn = _qkv(x, mix_norm_g[i], sh_a, sc_a, kv_norm_g, kv_sh, kv_sc,
                              bf(sb_w_q[j]), bf(w_kv), q_norm_g[j], k_norm_g)
            if k is None:
                k, v = kn, vn
            o = _attn(qn, k, v)
            x = _post(o, x, bf(sb_w_o[j]), g_a, mlp_norm_g[i], sh_m, sc_m, g_m,
                      bf(mlp_w1[i]), bf(mlp_w2[i]), glu=False)
    return x
```

```python
import functools
import math

import jax
import jax.numpy as jnp
from jax import lax
from jax.experimental import pallas as pl
from jax.experimental.pallas import tpu as pltpu

F32 = jnp.float32
BF16 = jnp.bfloat16

EPS = 1e-6
HEAD_DIM = 64
S5_GROUP = 16
S5_STATE = 64

V7X_LANES = 128
V7X_MXU_DIM = 256
V7X_VMEM_BYTES = 64 * 1024 * 1024
VMEM_LIMIT_BYTES = V7X_VMEM_BYTES - 8 * 1024 * 1024

S5_CH_PER_TILE = V7X_MXU_DIM
S5_GROUPS_PER_TILE = S5_CH_PER_TILE // S5_GROUP
S5_CPLX_PER_TILE = S5_GROUPS_PER_TILE * S5_STATE
S5_COLS_PER_TILE = 2 * S5_CPLX_PER_TILE


def _params(sem, vmem=VMEM_LIMIT_BYTES):
    return pltpu.CompilerParams(dimension_semantics=sem, vmem_limit_bytes=vmem)


def _dot(a, b):
    return jnp.dot(a, b, preferred_element_type=F32)


def _rms(x):
    return x * lax.rsqrt(jnp.mean(x * x, axis=-1, keepdims=True) + EPS)


def _ada_body(c_ref, w_ref, b_ref, o_ref):
    c = c_ref[...]
    s = c * jax.nn.sigmoid(c)
    o_ref[...] = jnp.dot(s, w_ref[...], preferred_element_type=F32,
                         precision=lax.Precision.HIGHEST) + b_ref[...]


def _ada(c, w, b, tn=1024):
    nl, d, e = w.shape
    nb = c.shape[0]
    return pl.pallas_call(
        _ada_body,
        grid=(nl, e // tn),
        in_specs=[pl.BlockSpec((nb, d), lambda l, n: (0, 0)),
                  pl.BlockSpec((None, d, tn), lambda l, n: (l, 0, n)),
                  pl.BlockSpec((None, 1, tn), lambda l, n: (l, 0, n))],
        out_specs=pl.BlockSpec((None, nb, tn), lambda l, n: (l, 0, n)),
        out_shape=jax.ShapeDtypeStruct((nl, nb, e), F32),
        compiler_params=_params(("arbitrary", "arbitrary")),
        name="ada",
    )(c, w, b.reshape(nl, 1, e))


def _s5param_body(lre_ref, lim_ref, dt_ref, bre_ref, bim_ref, are_ref, aim_ref, bbre_ref, bbim_ref):
    lam_re = lre_ref[...]
    lam_im = lim_ref[...]
    dt = jnp.exp(dt_ref[...])
    mag = jnp.exp(lam_re * dt)
    ab_re = mag * jnp.cos(lam_im * dt)
    ab_im = mag * jnp.sin(lam_im * dt)
    den = lam_re * lam_re + lam_im * lam_im
    nr = ab_re - 1
    ni = ab_im
    f_re = (nr * lam_re + ni * lam_im) / den
    f_im = (ni * lam_re - nr * lam_im) / den
    br = bre_ref[...]
    bi = bim_ref[...]
    are_ref[...] = ab_re
    aim_ref[...] = ab_im
    bbre_ref[...] = f_re * br - f_im * bi
    bbim_ref[...] = f_re * bi + f_im * br


def _s5param(a_re, a_im, log_dt, b_re, b_im):
    rows, h = b_re.shape
    col = jax.ShapeDtypeStruct((rows, 1), F32)
    mat = jax.ShapeDtypeStruct((rows, h), F32)
    return pl.pallas_call(_s5param_body, out_shape=(col, col, mat, mat),
                          compiler_params=_params(None), name="s5param")(
        a_re, a_im, log_dt, b_re, b_im)


def _s5_body(x_ref, g_ref, sh_ref, sc_ref, perm_ref, permt_ref, wb_ref, are_ref, aim_ref, wc_ref,
             d_ref, gy_ref, u_ref, ut_ref, bu_ref, sbf_ref, st_ref, *, nb, tt, nsub, wcol):
    @pl.when(pl.program_id(0) == 0)
    def _():
        st_ref[...] = jnp.zeros_like(st_ref)

    d = x_ref.shape[-1]
    n_tiles = d // S5_CH_PER_TILE
    permt = permt_ref[...]
    rows_sub = nb * tt
    for sub in range(nsub):
        ts = slice(sub * tt, (sub + 1) * tt)
        rs = slice(sub * rows_sub, (sub + 1) * rows_sub)
        u_ref[rs] = ((_rms(x_ref[:, ts, :]) * g_ref[...]) * (1 + sc_ref[...]) + sh_ref[...]
                     ).reshape(rows_sub, d)
        ut_ref[rs] = _dot(perm_ref[...], u_ref[rs].astype(BF16)).astype(BF16)

    n_cc = S5_CPLX_PER_TILE // wcol
    for kc in range(n_tiles):
        ch = slice(kc * S5_CH_PER_TILE, (kc + 1) * S5_CH_PER_TILE)
        bu_ref[kc] = _dot(ut_ref[:, ch], wb_ref[kc])

        for cc in range(n_cc):
            q0 = kc * S5_CPLX_PER_TILE + cc * wcol
            cre = slice(cc * wcol, (cc + 1) * wcol)
            cim = slice(cre.start + S5_CPLX_PER_TILE, cre.stop + S5_CPLX_PER_TILE)
            slab = 2 * (kc * n_cc + cc)
            ar = jnp.broadcast_to(are_ref[:, q0:q0 + wcol], (nb, wcol))
            ai = jnp.broadcast_to(aim_ref[:, q0:q0 + wcol], (nb, wcol))
            sr = st_ref[slab]
            si = st_ref[slab + 1]
            for t in range(nsub * tt):
                rows = slice(t * nb, (t + 1) * nb)
                nsr = ar * sr - ai * si + bu_ref[kc, rows, cre]
                nsi = ar * si + ai * sr + bu_ref[kc, rows, cim]
                sr, si = nsr, nsi
                sbf_ref[kc, rows, cre] = sr.astype(BF16)
                sbf_ref[kc, rows, cim] = si.astype(BF16)
            st_ref[slab] = sr
            st_ref[slab + 1] = si

        y = _dot(sbf_ref[kc], wc_ref[kc])
        y_hi = y.astype(BF16)
        y_lo = (y - y_hi.astype(F32)).astype(BF16)
        for sub in range(nsub):
            ts = slice(sub * tt, (sub + 1) * tt)
            rs = slice(sub * rows_sub, (sub + 1) * rows_sub)
            yb = _dot(permt, y_hi[rs]) + _dot(permt, y_lo[rs]) + d_ref[:, ch] * u_ref[rs, ch]
            gy_ref[:, ts, ch] = jax.nn.gelu(yb).astype(BF16).reshape(nb, tt, S5_CH_PER_TILE)


def _s5(x, g, sh, sc, wb, are, aim, wc, d_skip, tt=16, nsub=2, wcol=512):
    nb, seq, d = x.shape
    rows = nb * tt
    n_tiles = d // S5_CH_PER_TILE
    n_cols = n_tiles * S5_COLS_PER_TILE
    r = jnp.arange(rows)
    perm = ((r[:, None] // nb == r[None, :] % tt) & (r[:, None] % nb == r[None, :] // tt))
    perm = perm.astype(BF16)
    const = lambda shape: pl.BlockSpec(shape, lambda t: (0,) * len(shape),
                                       pipeline_mode=pl.Buffered(1))
    tile = pl.BlockSpec((nb, nsub * tt, d), lambda t: (0, t, 0))
    body = functools.partial(_s5_body, nb=nb, tt=tt, nsub=nsub, wcol=wcol)
    return pl.pallas_call(
        body,
        grid=(seq // (nsub * tt),),
        in_specs=[tile, const((1, d)), const((nb, 1, d)), const((nb, 1, d)),
                  const((rows, rows)), const((rows, rows)),
                  const((n_tiles, S5_CH_PER_TILE, S5_COLS_PER_TILE)),
                  const((1, n_cols // 2)), const((1, n_cols // 2)),
                  const((n_tiles, S5_COLS_PER_TILE, S5_CH_PER_TILE)), const((1, d))],
        out_specs=tile,
        out_shape=jax.ShapeDtypeStruct((nb, seq, d), BF16),
        scratch_shapes=[pltpu.VMEM((nsub * rows, d), F32),
                        pltpu.VMEM((nsub * rows, d), BF16),
                        pltpu.VMEM((n_tiles, nsub * rows, S5_COLS_PER_TILE), F32),
                        pltpu.VMEM((n_tiles, nsub * rows, S5_COLS_PER_TILE), BF16),
                        pltpu.VMEM((n_cols // wcol, nb, wcol), F32)],
        compiler_params=_params(("arbitrary",)),
        name="s5",
    )(x, g.reshape(1, d), sh.reshape(nb, 1, d), sc.reshape(nb, 1, d), perm, perm.T, wb, are, aim,
      wc, d_skip.reshape(1, d))


def _s5_block_diag(bb_re, bb_im, c_re, c_im):
    g, p, h = bb_re.shape
    nt = g // S5_GROUPS_PER_TILE
    eye = jnp.eye(S5_GROUPS_PER_TILE, dtype=F32)

    def emb_b(bb):
        t = bb.reshape(nt, S5_GROUPS_PER_TILE, p, h)
        return jnp.einsum('kgph,gG->kghGp', t, eye).reshape(nt, S5_CH_PER_TILE, S5_CPLX_PER_TILE)

    def emb_c(cm):
        t = cm.reshape(nt, S5_GROUPS_PER_TILE, h, p)
        return jnp.einsum('kghp,gG->kgpGh', t, eye).reshape(nt, S5_CPLX_PER_TILE, S5_CH_PER_TILE)

    wb = jnp.concatenate([emb_b(bb_re), emb_b(bb_im)], axis=2).astype(BF16)
    wc = jnp.concatenate([emb_c(c_re), emb_c(-c_im)], axis=1).astype(BF16)
    return wb, wc


def _post_body(a_ref, x_ref, wmix_ref, ga_ref, gn_ref, shm_ref, scm_ref, gm_ref, w1_ref, w2_ref,
               o_ref, *, glu, fc):
    d = x_ref.shape[-1]
    mp = _dot(a_ref[...], wmix_ref[...])
    if glu:
        mix = mp[:, :d] * jax.nn.sigmoid(mp[:, d:])
    else:
        mix = mp
    x1 = x_ref[...] + ga_ref[...] * mix
    h = ((_rms(x1) * gn_ref[...]) * (1 + scm_ref[...]) + shm_ref[...]).astype(BF16)
    ff = None
    for c in range(w1_ref.shape[1] // fc):
        hh = jnp.square(jnp.maximum(_dot(h, w1_ref[:, c * fc:(c + 1) * fc]), 0)).astype(BF16)
        part = _dot(hh, w2_ref[c * fc:(c + 1) * fc, :])
        ff = part if ff is None else ff + part
    o_ref[...] = x1 + gm_ref[...] * ff


def _post(a, x, wmix, g_a, gn, sh_m, sc_m, g_m, w1, w2, glu, tm=512, fc=1024):
    nb, seq, d = x.shape
    tm = min(tm, seq)
    a_spec = pl.BlockSpec((None, tm, d), lambda b, t: (b, t, 0))
    vec = pl.BlockSpec((None, 1, d), lambda b, t: (b, 0, 0))
    const = lambda shape: pl.BlockSpec(shape, lambda b, t: (0,) * len(shape),
                                       pipeline_mode=pl.Buffered(1))
    r3 = lambda v: v.reshape(nb, 1, d)
    return pl.pallas_call(
        functools.partial(_post_body, glu=glu, fc=fc),
        grid=(nb, seq // tm),
        in_specs=[a_spec, pl.BlockSpec((None, tm, d), lambda b, t: (b, t, 0)),
                  const(wmix.shape), vec, const((1, d)), vec, vec, vec,
                  const(w1.shape), const(w2.shape)],
        out_specs=pl.BlockSpec((None, tm, d), lambda b, t: (b, t, 0)),
        out_shape=jax.ShapeDtypeStruct((nb, seq, d), F32),
        compiler_params=_params(("arbitrary", "arbitrary")),
        name="post_glu" if glu else "post_attn",
    )(a, x, wmix, r3(g_a), gn.reshape(1, d), r3(sh_m), r3(sc_m), r3(g_m), w1, w2)


def _qkv_body(x_ref, gq_ref, shq_ref, scq_ref, gkv_ref, shk_ref, sck_ref, wq_ref, wkv_ref,
              hones_ref, qg_ref, kg_ref, q_ref, k_ref, v_ref):
    d = x_ref.shape[-1]
    xn = _rms(x_ref[...])
    hq = ((xn * gq_ref[...]) * (1 + scq_ref[...]) + shq_ref[...]).astype(BF16)
    hkv = ((xn * gkv_ref[...]) * (1 + sck_ref[...]) + shk_ref[...]).astype(BF16)
    q = _dot(hq, wq_ref[...])
    kv = _dot(hkv, wkv_ref[...])
    hones = hones_ref[...]

    def head_norm(t, g):
        sq = (t * t).astype(BF16)
        parts = []
        for c in range(d // V7X_MXU_DIM):
            sl = slice(c * V7X_MXU_DIM, (c + 1) * V7X_MXU_DIM)
            parts.append(_dot(sq[:, sl], hones))
        ms = jnp.concatenate(parts, axis=1) * (1.0 / HEAD_DIM)
        return (t * lax.rsqrt(ms + EPS)) * g

    q_ref[...] = (head_norm(q, qg_ref[...]) * (1.0 / math.sqrt(HEAD_DIM))).astype(BF16)
    k_ref[...] = head_norm(kv[:, :d], kg_ref[...]).astype(BF16)
    v_ref[...] = kv[:, d:].astype(BF16)


def _qkv(x, gq, shq, scq, gkv, shk, sck, wq, wkv, qg, kg, tm=512):
    nb, seq, d = x.shape
    tm = min(tm, seq)
    vec = pl.BlockSpec((None, 1, d), lambda b, t: (b, 0, 0))
    const = lambda shape: pl.BlockSpec(shape, lambda b, t: (0,) * len(shape),
                                       pipeline_mode=pl.Buffered(1))
    tile = pl.BlockSpec((None, tm, d), lambda b, t: (b, t, 0))
    r3 = lambda v: v.reshape(nb, 1, d)
    idx = jnp.arange(V7X_MXU_DIM) // HEAD_DIM
    hones = (idx[:, None] == idx[None, :]).astype(BF16)
    n_heads = d // HEAD_DIM
    out = jax.ShapeDtypeStruct((nb, seq, d), BF16)
    return pl.pallas_call(
        _qkv_body,
        grid=(nb, seq // tm),
        in_specs=[tile, const((1, d)), vec, vec, const((1, d)), vec, vec,
                  const(wq.shape), const(wkv.shape), const(hones.shape), const((1, d)), const((1, d))],
        out_specs=(tile, tile, tile),
        out_shape=(out, out, out),
        compiler_params=_params(("arbitrary", "arbitrary")),
        name="qkv",
    )(x, gq.reshape(1, d), r3(shq), r3(scq), gkv.reshape(1, d), r3(shk), r3(sck), wq, wkv, hones,
      jnp.tile(qg, n_heads).reshape(1, d), jnp.tile(kg, n_heads).reshape(1, d))


ATTN_DONE_MASS = 104.0
ATTN_STEPS_PER_TRIP = 4


def _attn_body(q_ref, k_ref, v_ref, tri_ref, bias_ref, o_ref, qm_ref, vm_ref, c_ref, acc_ref,
               zb_ref, sp_ref, w_ref, qj_ref, qi_ref, *, tq, npp):
    seq = q_ref.shape[0]
    nblk = seq // tq
    head0 = lax.broadcasted_iota(jnp.int32, (tq, V7X_LANES), 1) < HEAD_DIM
    zero = jnp.zeros((tq, V7X_LANES), BF16)
    for pp in range(npp):
        lanes = slice(pp * V7X_LANES, (pp + 1) * V7X_LANES)
        for blk_i in range(nblk):
            rows = slice(blk_i * tq, (blk_i + 1) * tq)
            q = q_ref[rows, lanes]
            v = v_ref[rows, lanes]
            qm_ref[pp, blk_i, 0] = jnp.where(head0, q, zero)
            qm_ref[pp, blk_i, 1] = jnp.where(head0, zero, q)
            vm_ref[pp, blk_i, 0] = jnp.where(head0, v, zero)
            vm_ref[pp, blk_i, 1] = jnp.where(head0, zero, v)
    c_ref[...] = jnp.zeros_like(c_ref)
    acc_ref[...] = jnp.zeros_like(acc_ref)

    tri = tri_ref[...]
    dummy = nblk

    def score(par, j, i):
        ks = pl.ds(pl.multiple_of(j * tq, tq), tq)
        bias = bias_ref[jnp.asarray(j == i, jnp.int32)]
        for pp in range(npp):
            kj = k_ref[ks, pp * V7X_LANES:(pp + 1) * V7X_LANES]
            qi = qm_ref[pp, i].reshape(2 * tq, V7X_LANES)
            zb = lax.dot_general(qi, kj, (((1,), (1,)), ((), ())),
                                 preferred_element_type=F32) + bias
            sp = jnp.maximum(jnp.log(1 + jnp.exp(jnp.minimum(zb, 80.0))), zb)
            zb_ref[par, pp] = zb
            sp_ref[par, pp] = sp.astype(BF16)

    def mass(par, i, i_dst):
        cmin = None
        for pp in range(npp):
            r = _dot(sp_ref[par, pp], tri)
            c = c_ref[pp, i]
            w = jnp.exp(zb_ref[par, pp] - r - jnp.concatenate([c] * (tq // V7X_LANES), axis=1))
            w_ref[par, pp] = w.astype(BF16)
            c_new = c + jnp.broadcast_to(r[:, 0:1], (2 * tq, V7X_LANES))
            c_ref[pp, i_dst] = c_new
            cmin = c_new if cmin is None else jnp.minimum(cmin, c_new)
        return jnp.min(cmin)

    def out(par, j, i_dst):
        for pp in range(npp):
            w = w_ref[par, pp]
            pv = _dot(jnp.concatenate([w[:tq], w[tq:]], axis=1),
                      vm_ref[pp, j].reshape(2 * tq, V7X_LANES))
            acc_ref[pp, i_dst] += pv

    for e in range(nblk):
        qj_ref[e] = e
        qi_ref[e] = e
    score(0, 0, 0)
    score(1, 1, 1)
    mass(0, 0, 0)

    def cond(st):
        head, tail, _, _, live_m, _, _, live_o = st
        return (head < tail) | (live_m != 0) | (live_o != 0)

    def step(par, st):
        head, tail, jm, im, live_m, jo, io, live_o = st
        live_s = (head < tail).astype(jnp.int32)
        e = jnp.minimum(head, tail - 1)
        js = qj_ref[e]
        is_ = qi_ref[e]
        score(par, js, is_)
        cmin = mass(1 - par, im, jnp.where(live_m != 0, im, dummy))
        out(par, jo, jnp.where(live_o != 0, io, dummy))
        more = (live_m != 0) & (jm > 0) & (cmin <= ATTN_DONE_MASS)
        qj_ref[tail] = jm - 1
        qi_ref[tail] = im
        return (head + live_s, tail + more.astype(jnp.int32), js, is_, live_s, jm, im, live_m)

    def trip(st):
        for n in range(ATTN_STEPS_PER_TRIP):
            st = step(n % 2, st)
        return st

    one = jnp.int32(1)
    zero_i = jnp.int32(0)
    lax.while_loop(cond, trip,
                   (jnp.int32(2), jnp.int32(nblk), one, one, one, zero_i, zero_i, one))
    for pp in range(npp):
        o_ref[:, pp * V7X_LANES:(pp + 1) * V7X_LANES] = (
            acc_ref[pp, :nblk].reshape(seq, V7X_LANES).astype(BF16))


def _attn(q, k, v, tq=256, npp=1):
    nb, seq, d = q.shape
    tq = min(tq, seq)
    nblk = seq // tq
    assert nblk >= 2
    lanes = npp * V7X_LANES
    idx = jnp.arange(tq)
    tri = (idx[:, None] >= idx[None, :]).astype(BF16)
    strict = jnp.tile(idx[None, :] < idx[:, None], (2, 1))
    bias = jnp.stack([jnp.zeros((2 * tq, tq), F32), jnp.where(strict, 0.0, -1e30).astype(F32)])
    n_queue = nblk * (nblk + 1) // 2 + 1
    blk = pl.BlockSpec((None, seq, lanes), lambda b, p: (b, 0, p))
    return pl.pallas_call(
        functools.partial(_attn_body, tq=tq, npp=npp),
        grid=(nb, d // lanes),
        in_specs=[blk, blk, blk, pl.BlockSpec((tq, tq), lambda b, p: (0, 0)),
                  pl.BlockSpec((2, 2 * tq, tq), lambda b, p: (0, 0, 0))],
        out_specs=blk,
        scratch_shapes=[pltpu.VMEM((npp, nblk, 2, tq, V7X_LANES), BF16),
                        pltpu.VMEM((npp, nblk, 2, tq, V7X_LANES), BF16),
                        pltpu.VMEM((npp, nblk + 1, 2 * tq, V7X_LANES), F32),
                        pltpu.VMEM((npp, nblk + 1, tq, V7X_LANES), F32),
                        pltpu.VMEM((2, npp, 2 * tq, tq), F32),
                        pltpu.VMEM((2, npp, 2 * tq, tq), BF16),
                        pltpu.VMEM((2, npp, 2 * tq, tq), BF16),
                        pltpu.SMEM((n_queue,), jnp.int32),
                        pltpu.SMEM((n_queue,), jnp.int32)],
        out_shape=jax.ShapeDtypeStruct((nb, seq, d), BF16),
        compiler_params=_params(("arbitrary", "arbitrary")),
        name="attn",
    )(q, k, v, tri, bias)


def kernel(x, c, ada_w, ada_b, mix_norm_g, mlp_norm_g, mlp_w1, mlp_w2, s5_a_re, s5_a_im, s5_log_dt,
           s5_b_re, s5_b_im, s5_c_re, s5_c_im, s5_d, s5_w_glu, kv_ada_w, kv_ada_b, kv_norm_g, w_kv,
           k_norm_g, sb_w_q, q_norm_g, sb_w_o):
    nb, seq, d = x.shape
    depth = ada_w.shape[0]
    n_a = s5_a_re.shape[0]
    assert d % S5_CH_PER_TILE == 0 and nb % 16 == 0 and seq % 256 == 0

    mods = _ada(c, ada_w, ada_b)
    kvmod = _ada(c, kv_ada_w[None], kv_ada_b[None])[0]
    bf = lambda w: w.astype(BF16)

    q = k = v = None
    for i in range(depth):
        sh_a, sc_a, g_a, sh_m, sc_m, g_m = [mods[i][:, n * d:(n + 1) * d] for n in range(6)]
        if i < n_a:
            g, p = s5_a_re.shape[1:]
            rows = g * p
            dt_rows = jnp.repeat(s5_log_dt[i], p).reshape(rows, 1)
            are, aim, bb_re, bb_im = _s5param(
                s5_a_re[i].reshape(rows, 1), s5_a_im[i].reshape(rows, 1), dt_rows,
                s5_b_re[i].reshape(rows, S5_GROUP), s5_b_im[i].reshape(rows, S5_GROUP))
            wb, wc = _s5_block_diag(bb_re.reshape(g, p, S5_GROUP), bb_im.reshape(g, p, S5_GROUP),
                                    s5_c_re[i], s5_c_im[i])
            gy = _s5(x, mix_norm_g[i], sh_a, sc_a, wb, are.reshape(1, rows), aim.reshape(1, rows),
                     wc, s5_d[i])
            x = _post(gy, x, bf(s5_w_glu[i]), g_a, mlp_norm_g[i], sh_m, sc_m, g_m,
                      bf(mlp_w1[i]), bf(mlp_w2[i]), glu=True)
        else:
            j = i - n_a
            if j == 0:
                k = v = None
            kv_sh, kv_sc = kvmod[:, :d], kvmod[:, d:]
            qn, kn, vn = _qkv(x, mix_norm_g[i], sh_a, sc_a, kv_norm_g, kv_sh, kv_sc,
                              bf(sb_w_q[j]), bf(w_kv), q_norm_g[j], k_norm_g)
            if k is None:
                k, v = kn, vn
            o = _attn(qn, k, v)
            x = _post(o, x, bf(sb_w_o[j]), g_a, mlp_norm_g[i], sh_m, sc_m, g_m,
                      bf(mlp_w1[i]), bf(mlp_w2[i]), glu=False)
    return x
```

```python
import functools
import math

import jax
import jax.numpy as jnp
import numpy as np
from jax import lax
from jax.experimental import pallas as pl
from jax.experimental.pallas import tpu as pltpu

F32 = jnp.float32
BF16 = jnp.bfloat16

EPS = 1e-6
HEAD_DIM = 64
S5_GROUP = 16
S5_STATE = 64

V7X_LANES = 128
V7X_MXU_DIM = 256
V7X_VMEM_BYTES = 64 * 1024 * 1024
VMEM_LIMIT_BYTES = V7X_VMEM_BYTES - 8 * 1024 * 1024

S5_CH_PER_TILE = V7X_MXU_DIM
S5_GROUPS_PER_TILE = S5_CH_PER_TILE // S5_GROUP
S5_CPLX_PER_TILE = S5_GROUPS_PER_TILE * S5_STATE
S5_COLS_PER_TILE = 2 * S5_CPLX_PER_TILE


MOD_SHIFT_A, MOD_SCALE_A, MOD_GATE_A, MOD_SHIFT_M, MOD_SCALE_M, MOD_GATE_M = range(6)
MOD_SHIFT_KV, MOD_SCALE_KV = range(2)


def _params(sem, vmem=VMEM_LIMIT_BYTES):
    return pltpu.CompilerParams(dimension_semantics=sem, vmem_limit_bytes=vmem)


def _dot(a, b):
    return jnp.dot(a, b, preferred_element_type=F32)


def _rms(x):
    return x * lax.rsqrt(jnp.mean(x * x, axis=-1, keepdims=True) + EPS)


def _ada_body(c_ref, w_ref, b_ref, o_ref):
    c = c_ref[...]
    s = c * jax.nn.sigmoid(c)
    o_ref[...] = jnp.dot(s, w_ref[...], preferred_element_type=F32,
                         precision=lax.Precision.HIGHEST) + b_ref[...]


def _ada(c, w, b):
    nl, d, e = w.shape
    nb = c.shape[0]
    return pl.pallas_call(
        _ada_body,
        grid=(nl, e // d),
        in_specs=[pl.BlockSpec((nb, d), lambda l, n: (0, 0)),
                  pl.BlockSpec((None, d, d), lambda l, n: (l, 0, n)),
                  pl.BlockSpec((None, 1, d), lambda l, n: (l, 0, n))],
        out_specs=pl.BlockSpec((None, None, nb, d), lambda l, n: (l, n, 0, 0)),
        out_shape=jax.ShapeDtypeStruct((nl, e // d, nb, d), F32),
        compiler_params=_params(("arbitrary", "arbitrary")),
        name="ada",
    )(c, w, b.reshape(nl, 1, e))


def _mod_spec(mods, layer):
    _, n, nb, d = mods.shape
    return pl.BlockSpec((None, n, nb, d), lambda *_: (layer, 0, 0, 0),
                        pipeline_mode=pl.Buffered(1))


def _s5param_body(lre_ref, lim_ref, dt_ref, bre_ref, bim_ref, are_ref, aim_ref, bbre_ref, bbim_ref):
    lam_re = lre_ref[...]
    lam_im = lim_ref[...]
    dt = jnp.exp(dt_ref[...])
    mag = jnp.exp(lam_re * dt)
    ab_re = mag * jnp.cos(lam_im * dt)
    ab_im = mag * jnp.sin(lam_im * dt)
    den = lam_re * lam_re + lam_im * lam_im
    nr = ab_re - 1
    ni = ab_im
    f_re = (nr * lam_re + ni * lam_im) / den
    f_im = (ni * lam_re - nr * lam_im) / den
    br = bre_ref[...]
    bi = bim_ref[...]
    are_ref[...] = ab_re
    aim_ref[...] = ab_im
    bbre_ref[...] = f_re * br - f_im * bi
    bbim_ref[...] = f_re * bi + f_im * br


def _s5param(a_re, a_im, log_dt, b_re, b_im):
    rows, h = b_re.shape
    col = jax.ShapeDtypeStruct((rows, 1), F32)
    mat = jax.ShapeDtypeStruct((rows, h), F32)
    return pl.pallas_call(_s5param_body, out_shape=(col, col, mat, mat),
                          compiler_params=_params(None), name="s5param")(
        a_re, a_im, log_dt, b_re, b_im)


def _s5_body(x_ref, g_ref, mods_ref, perm_ref, permt_ref, wb_ref, are_ref, aim_ref, wc_ref,
             d_ref, gy_ref, u_ref, ut_ref, bu_ref, sbf_ref, st_ref, *, nb, tt, nsub, wcol):
    @pl.when(pl.program_id(0) == 0)
    def _():
        st_ref[...] = jnp.zeros_like(st_ref)

    d = x_ref.shape[-1]
    n_tiles = d // S5_CH_PER_TILE
    permt = permt_ref[...]
    rows_sub = nb * tt
    for sub in range(nsub):
        ts = slice(sub * tt, (sub + 1) * tt)
        rs = slice(sub * rows_sub, (sub + 1) * rows_sub)
        for b in range(nb):
            u_ref[rs.start + b * tt:rs.start + (b + 1) * tt] = (
                (_rms(x_ref[b, ts, :]) * g_ref[...]) * (1 + mods_ref[MOD_SCALE_A, b:b + 1, :])
                + mods_ref[MOD_SHIFT_A, b:b + 1, :])
        ut_ref[rs] = _dot(perm_ref[...], u_ref[rs].astype(BF16)).astype(BF16)

    n_cc = S5_CPLX_PER_TILE // wcol

    def project_in(kc):
        ch = slice(kc * S5_CH_PER_TILE, (kc + 1) * S5_CH_PER_TILE)
        bu_ref[kc] = _dot(ut_ref[:, ch], wb_ref[kc])

    def recur(kc, ccs):
        for cc in ccs:
            q0 = kc * S5_CPLX_PER_TILE + cc * wcol
            cre = slice(cc * wcol, (cc + 1) * wcol)
            cim = slice(cre.start + S5_CPLX_PER_TILE, cre.stop + S5_CPLX_PER_TILE)
            slab = 2 * (kc * n_cc + cc)
            ar = jnp.broadcast_to(are_ref[:, q0:q0 + wcol], (nb, wcol))
            ai = jnp.broadcast_to(aim_ref[:, q0:q0 + wcol], (nb, wcol))
            sr = st_ref[slab]
            si = st_ref[slab + 1]
            for t in range(nsub * tt):
                rows = slice(t * nb, (t + 1) * nb)
                nsr = ar * sr - ai * si + bu_ref[kc, rows, cre]
                nsi = ar * si + ai * sr + bu_ref[kc, rows, cim]
                sr, si = nsr, nsi
                sbf_ref[kc, rows, cre] = sr.astype(BF16)
                sbf_ref[kc, rows, cim] = si.astype(BF16)
            st_ref[slab] = sr
            st_ref[slab + 1] = si

    def project_out(kc):
        ch = slice(kc * S5_CH_PER_TILE, (kc + 1) * S5_CH_PER_TILE)
        y = _dot(sbf_ref[kc], wc_ref[kc])
        y_hi = y.astype(BF16)
        y_lo = (y - y_hi.astype(F32)).astype(BF16)
        for sub in range(nsub):
            ts = slice(sub * tt, (sub + 1) * tt)
            rs = slice(sub * rows_sub, (sub + 1) * rows_sub)
            yb = _dot(permt, y_hi[rs]) + _dot(permt, y_lo[rs]) + d_ref[:, ch] * u_ref[rs, ch]
            gy_ref[:, ts, ch] = jax.nn.gelu(yb).astype(BF16).reshape(nb, tt, S5_CH_PER_TILE)

    project_in(0)
    for kc in range(n_tiles):
        if kc + 1 < n_tiles:
            project_in(kc + 1)
        recur(kc, range(n_cc))
        if kc > 0:
            project_out(kc - 1)
    project_out(n_tiles - 1)


def _s5(x, g, mods, layer, wb, are, aim, wc, d_skip, tt=16, nsub=2, wcol=512):
    nb, seq, d = x.shape
    rows = nb * tt
    n_tiles = d // S5_CH_PER_TILE
    n_cols = n_tiles * S5_COLS_PER_TILE
    r = np.arange(rows)
    perm = (r[:, None] // nb == r[None, :] % tt) & (r[:, None] % nb == r[None, :] // tt)
    permt = jnp.asarray(perm.T, BF16)
    perm = jnp.asarray(perm, BF16)
    const = lambda shape: pl.BlockSpec(shape, lambda t: (0,) * len(shape),
                                       pipeline_mode=pl.Buffered(1))
    tile = pl.BlockSpec((nb, nsub * tt, d), lambda t: (0, t, 0))
    body = functools.partial(_s5_body, nb=nb, tt=tt, nsub=nsub, wcol=wcol)
    return pl.pallas_call(
        body,
        grid=(seq // (nsub * tt),),
        in_specs=[tile, const((1, d)), _mod_spec(mods, layer),
                  const((rows, rows)), const((rows, rows)),
                  const((n_tiles, S5_CH_PER_TILE, S5_COLS_PER_TILE)),
                  const((1, n_cols // 2)), const((1, n_cols // 2)),
                  const((n_tiles, S5_COLS_PER_TILE, S5_CH_PER_TILE)), const((1, d))],
        out_specs=tile,
        out_shape=jax.ShapeDtypeStruct((nb, seq, d), BF16),
        scratch_shapes=[pltpu.VMEM((nsub * rows, d), F32),
                        pltpu.VMEM((nsub * rows, d), BF16),
                        pltpu.VMEM((n_tiles, nsub * rows, S5_COLS_PER_TILE), F32),
                        pltpu.VMEM((n_tiles, nsub * rows, S5_COLS_PER_TILE), BF16),
                        pltpu.VMEM((n_cols // wcol, nb, wcol), F32)],
        compiler_params=_params(("arbitrary",)),
        name="s5",
    )(x, g.reshape(1, d), mods, perm, permt, wb, are, aim, wc, d_skip.reshape(1, d))


def _s5_block_diag(bb_re, bb_im, c_re, c_im):
    g, p, h = bb_re.shape
    nt = g // S5_GROUPS_PER_TILE
    eye = jnp.eye(S5_GROUPS_PER_TILE, dtype=F32)

    def emb_b(bb):
        t = bb.reshape(nt, S5_GROUPS_PER_TILE, p, h)
        return jnp.einsum('kgph,gG->kghGp', t, eye).reshape(nt, S5_CH_PER_TILE, S5_CPLX_PER_TILE)

    def emb_c(cm):
        t = cm.reshape(nt, S5_GROUPS_PER_TILE, h, p)
        return jnp.einsum('kghp,gG->kgpGh', t, eye).reshape(nt, S5_CPLX_PER_TILE, S5_CH_PER_TILE)

    wb = jnp.concatenate([emb_b(bb_re), emb_b(bb_im)], axis=2).astype(BF16)
    wc = jnp.concatenate([emb_c(c_re), emb_c(-c_im)], axis=1).astype(BF16)
    return wb, wc


def _post_body(a_ref, x_ref, wmix_ref, mods_ref, gn_ref, w1_ref, w2_ref, o_ref, *, glu, fc):
    d = x_ref.shape[-1]
    row = pl.ds(pl.program_id(0), 1)
    mp = _dot(a_ref[...], wmix_ref[...])
    if glu:
        mix = mp[:, :d] * jax.nn.sigmoid(mp[:, d:])
    else:
        mix = mp
    x1 = x_ref[...] + mods_ref[MOD_GATE_A, row, :] * mix
    h = ((_rms(x1) * gn_ref[...]) * (1 + mods_ref[MOD_SCALE_M, row, :])
         + mods_ref[MOD_SHIFT_M, row, :]).astype(BF16)
    ff = None
    for c in range(w1_ref.shape[1] // fc):
        hh = jnp.square(jnp.maximum(_dot(h, w1_ref[:, c * fc:(c + 1) * fc]), 0)).astype(BF16)
        part = _dot(hh, w2_ref[c * fc:(c + 1) * fc, :])
        ff = part if ff is None else ff + part
    o_ref[...] = x1 + mods_ref[MOD_GATE_M, row, :] * ff


def _post(a, x, wmix, mods, layer, gn, w1, w2, glu, tm=512, fc=1024):
    nb, seq, d = x.shape
    tm = min(tm, seq)
    tile = pl.BlockSpec((None, tm, d), lambda b, t: (b, t, 0))
    const = lambda shape: pl.BlockSpec(shape, lambda b, t: (0,) * len(shape),
                                       pipeline_mode=pl.Buffered(1))
    return pl.pallas_call(
        functools.partial(_post_body, glu=glu, fc=fc),
        grid=(nb, seq // tm),
        in_specs=[tile, tile, const(wmix.shape), _mod_spec(mods, layer), const((1, d)),
                  const(w1.shape), const(w2.shape)],
        out_specs=tile,
        out_shape=jax.ShapeDtypeStruct((nb, seq, d), F32),
        compiler_params=_params(("arbitrary", "arbitrary")),
        name="post_glu" if glu else "post_attn",
    )(a, x, wmix, mods, gn.reshape(1, d), w1, w2)


def _qkv_body(x_ref, gq_ref, mods_ref, gkv_ref, kvmod_ref, wq_ref, wkv_ref, hones_ref, qg_ref,
              kg_ref, q_ref, k_ref, v_ref):
    d = x_ref.shape[-1]
    row = pl.ds(pl.program_id(0), 1)
    xn = _rms(x_ref[...])
    hq = ((xn * gq_ref[...]) * (1 + mods_ref[MOD_SCALE_A, row, :])
          + mods_ref[MOD_SHIFT_A, row, :]).astype(BF16)
    hkv = ((xn * gkv_ref[...]) * (1 + kvmod_ref[MOD_SCALE_KV, row, :])
           + kvmod_ref[MOD_SHIFT_KV, row, :]).astype(BF16)
    q = _dot(hq, wq_ref[...])
    kv = _dot(hkv, wkv_ref[...])
    hones = hones_ref[...]

    def head_norm(t, g):
        sq = (t * t).astype(BF16)
        parts = []
        for c in range(d // V7X_MXU_DIM):
            sl = slice(c * V7X_MXU_DIM, (c + 1) * V7X_MXU_DIM)
            parts.append(_dot(sq[:, sl], hones))
        ms = jnp.concatenate(parts, axis=1) * (1.0 / HEAD_DIM)
        return (t * lax.rsqrt(ms + EPS)) * g

    q_ref[...] = (head_norm(q, qg_ref[...]) * (1.0 / math.sqrt(HEAD_DIM))).astype(BF16)
    k_ref[...] = head_norm(kv[:, :d], kg_ref[...]).astype(BF16)
    v_ref[...] = kv[:, d:].astype(BF16)


def _qkv(x, gq, mods, layer, gkv, kvmod, wq, wkv, qg, kg, tm=512):
    nb, seq, d = x.shape
    tm = min(tm, seq)
    const = lambda shape: pl.BlockSpec(shape, lambda b, t: (0,) * len(shape),
                                       pipeline_mode=pl.Buffered(1))
    tile = pl.BlockSpec((None, tm, d), lambda b, t: (b, t, 0))
    idx = np.arange(V7X_MXU_DIM) // HEAD_DIM
    hones = jnp.asarray(idx[:, None] == idx[None, :], BF16)
    n_heads = d // HEAD_DIM
    out = jax.ShapeDtypeStruct((nb, seq, d), BF16)
    return pl.pallas_call(
        _qkv_body,
        grid=(nb, seq // tm),
        in_specs=[tile, const((1, d)), _mod_spec(mods, layer), const((1, d)), _mod_spec(kvmod, 0),
                  const(wq.shape), const(wkv.shape), const(hones.shape), const((1, d)),
                  const((1, d))],
        out_specs=(tile, tile, tile),
        out_shape=(out, out, out),
        compiler_params=_params(("arbitrary", "arbitrary")),
        name="qkv",
    )(x, gq.reshape(1, d), mods, gkv.reshape(1, d), kvmod, wq, wkv, hones,
      jnp.tile(qg, n_heads).reshape(1, d), jnp.tile(kg, n_heads).reshape(1, d))


ATTN_DONE_MASS = 104.0
ATTN_STEPS_PER_TRIP = 4


def _attn_body(q_ref, k_ref, v_ref, tri_ref, bias_ref, o_ref, qm_ref, vm_ref, c_ref, acc_ref,
               zb_ref, sp_ref, w_ref, qj_ref, qi_ref, *, tq, npp):
    seq = q_ref.shape[0]
    nblk = seq // tq
    head0 = lax.broadcasted_iota(jnp.int32, (tq, V7X_LANES), 1) < HEAD_DIM
    zero = jnp.zeros((tq, V7X_LANES), BF16)
    for pp in range(npp):
        lanes = slice(pp * V7X_LANES, (pp + 1) * V7X_LANES)
        for blk_i in range(nblk):
            rows = slice(blk_i * tq, (blk_i + 1) * tq)
            q = q_ref[rows, lanes]
            v = v_ref[rows, lanes]
            qm_ref[pp, blk_i, 0] = jnp.where(head0, q, zero)
            qm_ref[pp, blk_i, 1] = jnp.where(head0, zero, q)
            vm_ref[pp, blk_i, 0] = jnp.where(head0, v, zero)
            vm_ref[pp, blk_i, 1] = jnp.where(head0, zero, v)
    c_ref[...] = jnp.zeros_like(c_ref)
    acc_ref[...] = jnp.zeros_like(acc_ref)

    tri = tri_ref[...]
    dummy = nblk

    def score(par, j, i):
        ks = pl.ds(pl.multiple_of(j * tq, tq), tq)
        bias = bias_ref[jnp.asarray(j == i, jnp.int32)]
        for pp in range(npp):
            kj = k_ref[ks, pp * V7X_LANES:(pp + 1) * V7X_LANES]
            qi = qm_ref[pp, i].reshape(2 * tq, V7X_LANES)
            zb = lax.dot_general(qi, kj, (((1,), (1,)), ((), ())),
                                 preferred_element_type=F32) + bias
            sp = jnp.maximum(jnp.log(1 + jnp.exp(jnp.minimum(zb, 80.0))), zb)
            zb_ref[par, pp] = zb
            sp_ref[par, pp] = sp.astype(BF16)

    def mass(par, i, i_dst):
        cmin = None
        for pp in range(npp):
            r = _dot(sp_ref[par, pp], tri)
            c = c_ref[pp, i]
            w = jnp.exp(zb_ref[par, pp] - r - jnp.concatenate([c] * (tq // V7X_LANES), axis=1))
            w_ref[par, pp] = w.astype(BF16)
            c_new = c + jnp.broadcast_to(r[:, 0:1], (2 * tq, V7X_LANES))
            c_ref[pp, i_dst] = c_new
            cmin = c_new if cmin is None else jnp.minimum(cmin, c_new)
        return jnp.min(cmin)

    def out(par, j, i_dst):
        for pp in range(npp):
            w = w_ref[par, pp]
            pv = _dot(jnp.concatenate([w[:tq], w[tq:]], axis=1),
                      vm_ref[pp, j].reshape(2 * tq, V7X_LANES))
            acc_ref[pp, i_dst] += pv

    for e in range(nblk):
        qj_ref[e] = e
        qi_ref[e] = e
    score(0, 0, 0)
    score(1, 1, 1)
    mass(0, 0, 0)

    def cond(st):
        head, tail, _, _, live_m, _, _, live_o = st
        return (head < tail) | (live_m != 0) | (live_o != 0)

    def step(par, st):
        head, tail, jm, im, live_m, jo, io, live_o = st
        live_s = (head < tail).astype(jnp.int32)
        e = jnp.minimum(head, tail - 1)
        js = qj_ref[e]
        is_ = qi_ref[e]
        score(par, js, is_)
        cmin = mass(1 - par, im, jnp.where(live_m != 0, im, dummy))
        out(par, jo, jnp.where(live_o != 0, io, dummy))
        more = (live_m != 0) & (jm > 0) & (cmin <= ATTN_DONE_MASS)
        qj_ref[tail] = jm - 1
        qi_ref[tail] = im
        return (head + live_s, tail + more.astype(jnp.int32), js, is_, live_s, jm, im, live_m)

    def trip(st):
        for n in range(ATTN_STEPS_PER_TRIP):
            st = step(n % 2, st)
        return st

    one = jnp.int32(1)
    zero_i = jnp.int32(0)
    lax.while_loop(cond, trip,
                   (jnp.int32(2), jnp.int32(nblk), one, one, one, zero_i, zero_i, one))
    for pp in range(npp):
        o_ref[:, pp * V7X_LANES:(pp + 1) * V7X_LANES] = (
            acc_ref[pp, :nblk].reshape(seq, V7X_LANES).astype(BF16))


def _attn(q, k, v, tq=256, npp=1):
    nb, seq, d = q.shape
    tq = min(tq, seq)
    nblk = seq // tq
    assert nblk >= 2 and ATTN_STEPS_PER_TRIP % 2 == 0
    lanes = npp * V7X_LANES
    idx = np.arange(tq)
    tri = jnp.asarray(idx[:, None] >= idx[None, :], BF16)
    strict = np.tile(idx[None, :] < idx[:, None], (2, 1))
    bias = jnp.asarray(np.stack([np.zeros((2 * tq, tq)), np.where(strict, 0.0, -1e30)]), F32)
    n_queue = nblk * (nblk + 1) // 2 + 1
    blk = pl.BlockSpec((None, seq, lanes), lambda b, p: (b, 0, p))
    return pl.pallas_call(
        functools.partial(_attn_body, tq=tq, npp=npp),
        grid=(nb, d // lanes),
        in_specs=[blk, blk, blk, pl.BlockSpec((tq, tq), lambda b, p: (0, 0)),
                  pl.BlockSpec((2, 2 * tq, tq), lambda b, p: (0, 0, 0))],
        out_specs=blk,
        scratch_shapes=[pltpu.VMEM((npp, nblk, 2, tq, V7X_LANES), BF16),
                        pltpu.VMEM((npp, nblk, 2, tq, V7X_LANES), BF16),
                        pltpu.VMEM((npp, nblk + 1, 2 * tq, V7X_LANES), F32),
                        pltpu.VMEM((npp, nblk + 1, tq, V7X_LANES), F32),
                        pltpu.VMEM((2, npp, 2 * tq, tq), F32),
                        pltpu.VMEM((2, npp, 2 * tq, tq), BF16),
                        pltpu.VMEM((2, npp, 2 * tq, tq), BF16),
                        pltpu.SMEM((n_queue,), jnp.int32),
                        pltpu.SMEM((n_queue,), jnp.int32)],
        out_shape=jax.ShapeDtypeStruct((nb, seq, d), BF16),
        compiler_params=_params(("arbitrary", "arbitrary")),
        name="attn",
    )(q, k, v, tri, bias)


def kernel(x, c, ada_w, ada_b, mix_norm_g, mlp_norm_g, mlp_w1, mlp_w2, s5_a_re, s5_a_im, s5_log_dt,
           s5_b_re, s5_b_im, s5_c_re, s5_c_im, s5_d, s5_w_glu, kv_ada_w, kv_ada_b, kv_norm_g, w_kv,
           k_norm_g, sb_w_q, q_norm_g, sb_w_o):
    nb, seq, d = x.shape
    depth = ada_w.shape[0]
    n_a = s5_a_re.shape[0]
    assert d % S5_CH_PER_TILE == 0 and nb % 16 == 0 and seq % 256 == 0

    mods = _ada(c, ada_w, ada_b)
    kvmod = _ada(c, kv_ada_w[None], kv_ada_b[None])
    bf = lambda w: w.astype(BF16)

    q = k = v = None
    for i in range(depth):
        if i < n_a:
            g, p = s5_a_re.shape[1:]
            rows = g * p
            dt_rows = jnp.repeat(s5_log_dt[i], p).reshape(rows, 1)
            are, aim, bb_re, bb_im = _s5param(
                s5_a_re[i].reshape(rows, 1), s5_a_im[i].reshape(rows, 1), dt_rows,
                s5_b_re[i].reshape(rows, S5_GROUP), s5_b_im[i].reshape(rows, S5_GROUP))
            wb, wc = _s5_block_diag(bb_re.reshape(g, p, S5_GROUP), bb_im.reshape(g, p, S5_GROUP),
                                    s5_c_re[i], s5_c_im[i])
            gy = _s5(x, mix_norm_g[i], mods, i, wb, are.reshape(1, rows), aim.reshape(1, rows),
                     wc, s5_d[i])
            x = _post(gy, x, bf(s5_w_glu[i]), mods, i, mlp_norm_g[i], bf(mlp_w1[i]),
                      bf(mlp_w2[i]), glu=True)
        else:
            j = i - n_a
            if j == 0:
                k = v = None
            qn, kn, vn = _qkv(x, mix_norm_g[i], mods, i, kv_norm_g, kvmod,
                              bf(sb_w_q[j]), bf(w_kv), q_norm_g[j], k_norm_g)
            if k is None:
                k, v = kn, vn
            o = _attn(qn, k, v)
            x = _post(o, x, bf(sb_w_o[j]), mods, i, mlp_norm_g[i], bf(mlp_w1[i]),
                      bf(mlp_w2[i]), glu=False)
    return x
```

```python
import functools
import math

import jax
import jax.numpy as jnp
import numpy as np
from jax import lax
from jax.experimental import pallas as pl
from jax.experimental.pallas import tpu as pltpu

F32 = jnp.float32
BF16 = jnp.bfloat16

EPS = 1e-6
HEAD_DIM = 64
S5_GROUP = 16
S5_STATE = 64

V7X_LANES = 128
V7X_MXU_DIM = 256
V7X_VMEM_BYTES = 64 * 1024 * 1024
VMEM_LIMIT_BYTES = V7X_VMEM_BYTES - 8 * 1024 * 1024

S5_CH_PER_TILE = V7X_LANES
S5_GROUPS_PER_TILE = S5_CH_PER_TILE // S5_GROUP
S5_CPLX_PER_TILE = S5_GROUPS_PER_TILE * S5_STATE
S5_COLS_PER_TILE = 2 * S5_CPLX_PER_TILE


MOD_SHIFT_A, MOD_SCALE_A, MOD_GATE_A, MOD_SHIFT_M, MOD_SCALE_M, MOD_GATE_M = range(6)
MOD_SHIFT_KV, MOD_SCALE_KV = range(2)


def _params(sem, vmem=VMEM_LIMIT_BYTES):
    return pltpu.CompilerParams(dimension_semantics=sem, vmem_limit_bytes=vmem)


def _dot(a, b):
    return jnp.dot(a, b, preferred_element_type=F32)


def _rms(x):
    return x * lax.rsqrt(jnp.mean(x * x, axis=-1, keepdims=True) + EPS)


def _ada_body(c_ref, w_ref, b_ref, o_ref):
    c = c_ref[...]
    s = c * jax.nn.sigmoid(c)
    o_ref[...] = jnp.dot(s, w_ref[...], preferred_element_type=F32,
                         precision=lax.Precision.HIGHEST) + b_ref[...]


def _ada(c, w, b):
    nl, d, e = w.shape
    nb = c.shape[0]
    return pl.pallas_call(
        _ada_body,
        grid=(nl, e // d),
        in_specs=[pl.BlockSpec((nb, d), lambda l, n: (0, 0)),
                  pl.BlockSpec((None, d, d), lambda l, n: (l, 0, n)),
                  pl.BlockSpec((None, 1, d), lambda l, n: (l, 0, n))],
        out_specs=pl.BlockSpec((None, None, nb, d), lambda l, n: (l, n, 0, 0)),
        out_shape=jax.ShapeDtypeStruct((nl, e // d, nb, d), F32),
        compiler_params=_params(("arbitrary", "arbitrary")),
        name="ada",
    )(c, w, b.reshape(nl, 1, e))


def _mod_spec(mods, layer):
    _, n, nb, d = mods.shape
    return pl.BlockSpec((None, n, nb, d), lambda *_: (layer, 0, 0, 0),
                        pipeline_mode=pl.Buffered(1))


def _s5param_body(lre_ref, lim_ref, dt_ref, bre_ref, bim_ref, are_ref, aim_ref, bbre_ref, bbim_ref):
    lam_re = lre_ref[...]
    lam_im = lim_ref[...]
    dt = jnp.exp(dt_ref[...])
    mag = jnp.exp(lam_re * dt)
    ab_re = mag * jnp.cos(lam_im * dt)
    ab_im = mag * jnp.sin(lam_im * dt)
    den = lam_re * lam_re + lam_im * lam_im
    nr = ab_re - 1
    ni = ab_im
    f_re = (nr * lam_re + ni * lam_im) / den
    f_im = (ni * lam_re - nr * lam_im) / den
    br = bre_ref[...]
    bi = bim_ref[...]
    are_ref[...] = ab_re
    aim_ref[...] = ab_im
    bbre_ref[...] = f_re * br - f_im * bi
    bbim_ref[...] = f_re * bi + f_im * br


def _s5param(a_re, a_im, log_dt, b_re, b_im):
    rows, h = b_re.shape
    col = jax.ShapeDtypeStruct((rows, 1), F32)
    mat = jax.ShapeDtypeStruct((rows, h), F32)
    return pl.pallas_call(_s5param_body, out_shape=(col, col, mat, mat),
                          compiler_params=_params(None), name="s5param")(
        a_re, a_im, log_dt, b_re, b_im)


def _s5_body(x_ref, g_ref, mods_ref, wb_ref, are_ref, aim_ref, wc_ref, d_ref, gy_ref,
             u_ref, ut_ref, bu_ref, sbf_ref, y_ref, st_ref, *, nb, tt, nsub):
    @pl.when(pl.program_id(0) == 0)
    def _():
        st_ref[...] = jnp.zeros_like(st_ref)

    d = x_ref.shape[-1]
    n_tiles = d // S5_CH_PER_TILE
    rows_sub = nb * tt
    for sub in range(nsub):
        ts = slice(sub * tt, (sub + 1) * tt)
        for b in range(nb):
            u = ((_rms(x_ref[b, ts, :]) * g_ref[...]) * (1 + mods_ref[MOD_SCALE_A, b:b + 1, :])
                 + mods_ref[MOD_SHIFT_A, b:b + 1, :])
            r0 = sub * rows_sub + b * tt
            for kc in range(n_tiles):
                u_ref[kc, r0:r0 + tt] = u[:, kc * S5_CH_PER_TILE:(kc + 1) * S5_CH_PER_TILE]

    def project_in(kc):
        for sub in range(nsub):
            for t in range(tt):
                ut_ref[kc, sub * rows_sub + t * nb:sub * rows_sub + (t + 1) * nb] = (
                    u_ref[kc, pl.ds(sub * rows_sub + t, nb, stride=tt)].astype(BF16))
        bu_ref[kc] = _dot(ut_ref[kc], wb_ref[kc])

    def recur(kc):
        cre = slice(0, S5_CPLX_PER_TILE)
        cim = slice(S5_CPLX_PER_TILE, S5_COLS_PER_TILE)
        q = slice(kc * S5_CPLX_PER_TILE, (kc + 1) * S5_CPLX_PER_TILE)
        ar = jnp.broadcast_to(are_ref[:, q], (nb, S5_CPLX_PER_TILE))
        ai = jnp.broadcast_to(aim_ref[:, q], (nb, S5_CPLX_PER_TILE))
        sr = st_ref[2 * kc]
        si = st_ref[2 * kc + 1]
        for t in range(nsub * tt):
            rows = slice(t * nb, (t + 1) * nb)
            nsr = ar * sr - ai * si + bu_ref[kc, rows, cre]
            nsi = ar * si + ai * sr + bu_ref[kc, rows, cim]
            sr, si = nsr, nsi
            sbf_ref[kc, rows, cre] = sr.astype(BF16)
            sbf_ref[kc, rows, cim] = si.astype(BF16)
        st_ref[2 * kc] = sr
        st_ref[2 * kc + 1] = si

    def project_out(kc):
        ch = slice(kc * S5_CH_PER_TILE, (kc + 1) * S5_CH_PER_TILE)
        y_ref[kc] = _dot(sbf_ref[kc], wc_ref[kc])
        for sub in range(nsub):
            ts = slice(sub * tt, (sub + 1) * tt)
            for b in range(nb):
                r0 = sub * rows_sub + b * tt
                y = (y_ref[kc, pl.ds(sub * rows_sub + b, tt, stride=nb)]
                     + d_ref[:, ch] * u_ref[kc, r0:r0 + tt])
                gy_ref[b, ts, ch] = jax.nn.gelu(y).astype(BF16)

    project_in(0)
    for kc in range(n_tiles):
        if kc + 1 < n_tiles:
            project_in(kc + 1)
        recur(kc)
        if kc > 0:
            project_out(kc - 1)
    project_out(n_tiles - 1)


def _s5(x, g, mods, layer, wb, are, aim, wc, d_skip, tt=16, nsub=2):
    nb, seq, d = x.shape
    rows = nsub * nb * tt
    n_tiles = d // S5_CH_PER_TILE
    n_cols = n_tiles * S5_COLS_PER_TILE
    const = lambda shape: pl.BlockSpec(shape, lambda t: (0,) * len(shape),
                                       pipeline_mode=pl.Buffered(1))
    tile = pl.BlockSpec((nb, nsub * tt, d), lambda t: (0, t, 0))
    body = functools.partial(_s5_body, nb=nb, tt=tt, nsub=nsub)
    return pl.pallas_call(
        body,
        grid=(seq // (nsub * tt),),
        in_specs=[tile, const((1, d)), _mod_spec(mods, layer),
                  const((n_tiles, S5_CH_PER_TILE, S5_COLS_PER_TILE)),
                  const((1, n_cols // 2)), const((1, n_cols // 2)),
                  const((n_tiles, S5_COLS_PER_TILE, S5_CH_PER_TILE)), const((1, d))],
        out_specs=tile,
        out_shape=jax.ShapeDtypeStruct((nb, seq, d), BF16),
        scratch_shapes=[pltpu.VMEM((n_tiles, rows, S5_CH_PER_TILE), F32),
                        pltpu.VMEM((n_tiles, rows, S5_CH_PER_TILE), BF16),
                        pltpu.VMEM((n_tiles, rows, S5_COLS_PER_TILE), F32),
                        pltpu.VMEM((n_tiles, rows, S5_COLS_PER_TILE), BF16),
                        pltpu.VMEM((n_tiles, rows, S5_CH_PER_TILE), F32),
                        pltpu.VMEM((2 * n_tiles, nb, S5_CPLX_PER_TILE), F32)],
        compiler_params=_params(("arbitrary",)),
        name="s5",
    )(x, g.reshape(1, d), mods, wb, are, aim, wc, d_skip.reshape(1, d))


def _s5_block_diag(bb_re, bb_im, c_re, c_im):
    g, p, h = bb_re.shape
    nt = g // S5_GROUPS_PER_TILE
    eye = jnp.eye(S5_GROUPS_PER_TILE, dtype=F32)

    def emb_b(bb):
        t = bb.reshape(nt, S5_GROUPS_PER_TILE, p, h)
        return jnp.einsum('kgph,gG->kghGp', t, eye).reshape(nt, S5_CH_PER_TILE, S5_CPLX_PER_TILE)

    def emb_c(cm):
        t = cm.reshape(nt, S5_GROUPS_PER_TILE, h, p)
        return jnp.einsum('kghp,gG->kgpGh', t, eye).reshape(nt, S5_CPLX_PER_TILE, S5_CH_PER_TILE)

    wb = jnp.concatenate([emb_b(bb_re), emb_b(bb_im)], axis=2).astype(BF16)
    wc = jnp.concatenate([emb_c(c_re), emb_c(-c_im)], axis=1).astype(BF16)
    return wb, wc


def _post_body(a_ref, x_ref, wmix_ref, mods_ref, gn_ref, w1_ref, w2_ref, o_ref, *, glu, fc):
    d = x_ref.shape[-1]
    row = pl.ds(pl.program_id(0), 1)
    mp = _dot(a_ref[...], wmix_ref[...])
    if glu:
        mix = mp[:, :d] * jax.nn.sigmoid(mp[:, d:])
    else:
        mix = mp
    x1 = x_ref[...] + mods_ref[MOD_GATE_A, row, :] * mix
    h = ((_rms(x1) * gn_ref[...]) * (1 + mods_ref[MOD_SCALE_M, row, :])
         + mods_ref[MOD_SHIFT_M, row, :]).astype(BF16)
    ff = None
    for c in range(w1_ref.shape[1] // fc):
        hh = jnp.square(jnp.maximum(_dot(h, w1_ref[:, c * fc:(c + 1) * fc]), 0)).astype(BF16)
        part = _dot(hh, w2_ref[c * fc:(c + 1) * fc, :])
        ff = part if ff is None else ff + part
    o_ref[...] = x1 + mods_ref[MOD_GATE_M, row, :] * ff


def _post(a, x, wmix, mods, layer, gn, w1, w2, glu, tm=512, fc=1024):
    nb, seq, d = x.shape
    tm = min(tm, seq)
    tile = pl.BlockSpec((None, tm, d), lambda b, t: (b, t, 0))
    const = lambda shape: pl.BlockSpec(shape, lambda b, t: (0,) * len(shape),
                                       pipeline_mode=pl.Buffered(1))
    return pl.pallas_call(
        functools.partial(_post_body, glu=glu, fc=fc),
        grid=(nb, seq // tm),
        in_specs=[tile, tile, const(wmix.shape), _mod_spec(mods, layer), const((1, d)),
                  const(w1.shape), const(w2.shape)],
        out_specs=tile,
        out_shape=jax.ShapeDtypeStruct((nb, seq, d), F32),
        compiler_params=_params(("arbitrary", "arbitrary")),
        name="post_glu" if glu else "post_attn",
    )(a, x, wmix, mods, gn.reshape(1, d), w1, w2)


def _qkv_body(x_ref, gq_ref, mods_ref, gkv_ref, kvmod_ref, wq_ref, wkv_ref, hones_ref, qg_ref,
              kg_ref, q_ref, k_ref, v_ref):
    d = x_ref.shape[-1]
    row = pl.ds(pl.program_id(0), 1)
    xn = _rms(x_ref[...])
    hq = ((xn * gq_ref[...]) * (1 + mods_ref[MOD_SCALE_A, row, :])
          + mods_ref[MOD_SHIFT_A, row, :]).astype(BF16)
    hkv = ((xn * gkv_ref[...]) * (1 + kvmod_ref[MOD_SCALE_KV, row, :])
           + kvmod_ref[MOD_SHIFT_KV, row, :]).astype(BF16)
    q = _dot(hq, wq_ref[...])
    kv = _dot(hkv, wkv_ref[...])
    hones = hones_ref[...]

    def head_norm(t, g):
        sq = (t * t).astype(BF16)
        parts = []
        for c in range(d // V7X_MXU_DIM):
            sl = slice(c * V7X_MXU_DIM, (c + 1) * V7X_MXU_DIM)
            parts.append(_dot(sq[:, sl], hones))
        ms = jnp.concatenate(parts, axis=1) * (1.0 / HEAD_DIM)
        return (t * lax.rsqrt(ms + EPS)) * g

    q_ref[...] = (head_norm(q, qg_ref[...]) * (1.0 / math.sqrt(HEAD_DIM))).astype(BF16)
    k_ref[...] = head_norm(kv[:, :d], kg_ref[...]).astype(BF16)
    v_ref[...] = kv[:, d:].astype(BF16)


def _qkv(x, gq, mods, layer, gkv, kvmod, wq, wkv, qg, kg, tm=512):
    nb, seq, d = x.shape
    tm = min(tm, seq)
    const = lambda shape: pl.BlockSpec(shape, lambda b, t: (0,) * len(shape),
                                       pipeline_mode=pl.Buffered(1))
    tile = pl.BlockSpec((None, tm, d), lambda b, t: (b, t, 0))
    idx = np.arange(V7X_MXU_DIM) // HEAD_DIM
    hones = jnp.asarray(idx[:, None] == idx[None, :], BF16)
    n_heads = d // HEAD_DIM
    out = jax.ShapeDtypeStruct((nb, seq, d), BF16)
    return pl.pallas_call(
        _qkv_body,
        grid=(nb, seq // tm),
        in_specs=[tile, const((1, d)), _mod_spec(mods, layer), const((1, d)), _mod_spec(kvmod, 0),
                  const(wq.shape), const(wkv.shape), const(hones.shape), const((1, d)),
                  const((1, d))],
        out_specs=(tile, tile, tile),
        out_shape=(out, out, out),
        compiler_params=_params(("arbitrary", "arbitrary")),
        name="qkv",
    )(x, gq.reshape(1, d), mods, gkv.reshape(1, d), kvmod, wq, wkv, hones,
      jnp.tile(qg, n_heads).reshape(1, d), jnp.tile(kg, n_heads).reshape(1, d))


ATTN_DONE_MASS = 104.0
ATTN_STEPS_PER_TRIP = 8


def _attn_body(q_ref, k_ref, v_ref, tri_ref, bias_ref, o_ref, qm_ref, vm_ref, c_ref, acc_ref,
               zb_ref, sp_ref, w_ref, qj_ref, qi_ref, *, tq, npp):
    seq = q_ref.shape[0]
    nblk = seq // tq
    head0 = lax.broadcasted_iota(jnp.int32, (tq, V7X_LANES), 1) < HEAD_DIM
    zero = jnp.zeros((tq, V7X_LANES), BF16)
    for pp in range(npp):
        lanes = slice(pp * V7X_LANES, (pp + 1) * V7X_LANES)
        for blk_i in range(nblk):
            rows = slice(blk_i * tq, (blk_i + 1) * tq)
            q = q_ref[rows, lanes]
            v = v_ref[rows, lanes]
            qm_ref[pp, blk_i, 0] = jnp.where(head0, q, zero)
            qm_ref[pp, blk_i, 1] = jnp.where(head0, zero, q)
            vm_ref[pp, blk_i, 0] = jnp.where(head0, v, zero)
            vm_ref[pp, blk_i, 1] = jnp.where(head0, zero, v)
    c_ref[...] = jnp.zeros_like(c_ref)
    acc_ref[...] = jnp.zeros_like(acc_ref)

    tri = tri_ref[...]
    dummy = nblk

    def score(par, j, i):
        ks = pl.ds(pl.multiple_of(j * tq, tq), tq)
        bias = bias_ref[jnp.asarray(j == i, jnp.int32)]
        for pp in range(npp):
            kj = k_ref[ks, pp * V7X_LANES:(pp + 1) * V7X_LANES]
            qi = qm_ref[pp, i].reshape(2 * tq, V7X_LANES)
            zb = lax.dot_general(qi, kj, (((1,), (1,)), ((), ())),
                                 preferred_element_type=F32) + bias
            sp = jnp.maximum(jnp.log(1 + jnp.exp(jnp.minimum(zb, 80.0))), zb)
            zb_ref[par, pp] = zb
            sp_ref[par, pp] = sp.astype(BF16)

    def mass(par, i, i_dst):
        cmin = None
        for pp in range(npp):
            r = _dot(sp_ref[par, pp], tri)
            c = c_ref[pp, i]
            w = jnp.exp(zb_ref[par, pp] - r - jnp.concatenate([c] * (tq // V7X_LANES), axis=1))
            w_ref[par, pp] = w.astype(BF16)
            c_new = c + jnp.broadcast_to(r[:, 0:1], (2 * tq, V7X_LANES))
            c_ref[pp, i_dst] = c_new
            cmin = c_new if cmin is None else jnp.minimum(cmin, c_new)
        return jnp.min(cmin)

    def out(par, j, i_dst):
        for pp in range(npp):
            w = w_ref[par, pp]
            pv = _dot(jnp.concatenate([w[:tq], w[tq:]], axis=1),
                      vm_ref[pp, j].reshape(2 * tq, V7X_LANES))
            acc_ref[pp, i_dst] += pv

    for e in range(nblk):
        qj_ref[e] = e
        qi_ref[e] = e
    score(0, 0, 0)
    score(1, 1, 1)
    mass(0, 0, 0)

    def cond(st):
        head, tail, _, _, live_m, _, _, live_o = st
        return (head < tail) | (live_m != 0) | (live_o != 0)

    def step(par, st):
        head, tail, jm, im, live_m, jo, io, live_o = st
        live_s = (head < tail).astype(jnp.int32)
        e = jnp.minimum(head, tail - 1)
        js = qj_ref[e]
        is_ = qi_ref[e]
        score(par, js, is_)
        cmin = mass(1 - par, im, jnp.where(live_m != 0, im, dummy))
        out(par, jo, jnp.where(live_o != 0, io, dummy))
        more = (live_m != 0) & (jm > 0) & (cmin <= ATTN_DONE_MASS)
        qj_ref[tail] = jm - 1
        qi_ref[tail] = im
        return (head + live_s, tail + more.astype(jnp.int32), js, is_, live_s, jm, im, live_m)

    def trip(st):
        for n in range(ATTN_STEPS_PER_TRIP):
            st = step(n % 2, st)
        return st

    one = jnp.int32(1)
    zero_i = jnp.int32(0)
    lax.while_loop(cond, trip,
                   (jnp.int32(2), jnp.int32(nblk), one, one, one, zero_i, zero_i, one))
    for pp in range(npp):
        o_ref[:, pp * V7X_LANES:(pp + 1) * V7X_LANES] = (
            acc_ref[pp, :nblk].reshape(seq, V7X_LANES).astype(BF16))


def _attn(q, k, v, tq=256, npp=1):
    nb, seq, d = q.shape
    tq = min(tq, seq)
    nblk = seq // tq
    assert nblk >= 2 and ATTN_STEPS_PER_TRIP % 2 == 0
    lanes = npp * V7X_LANES
    idx = np.arange(tq)
    tri = jnp.asarray(idx[:, None] >= idx[None, :], BF16)
    strict = np.tile(idx[None, :] < idx[:, None], (2, 1))
    bias = jnp.asarray(np.stack([np.zeros((2 * tq, tq)), np.where(strict, 0.0, -1e30)]), F32)
    n_queue = nblk * (nblk + 1) // 2 + 1
    blk = pl.BlockSpec((None, seq, lanes), lambda b, p: (b, 0, p))
    return pl.pallas_call(
        functools.partial(_attn_body, tq=tq, npp=npp),
        grid=(nb, d // lanes),
        in_specs=[blk, blk, blk, pl.BlockSpec((tq, tq), lambda b, p: (0, 0)),
                  pl.BlockSpec((2, 2 * tq, tq), lambda b, p: (0, 0, 0))],
        out_specs=blk,
        scratch_shapes=[pltpu.VMEM((npp, nblk, 2, tq, V7X_LANES), BF16),
                        pltpu.VMEM((npp, nblk, 2, tq, V7X_LANES), BF16),
                        pltpu.VMEM((npp, nblk + 1, 2 * tq, V7X_LANES), F32),
                        pltpu.VMEM((npp, nblk + 1, tq, V7X_LANES), F32),
                        pltpu.VMEM((2, npp, 2 * tq, tq), F32),
                        pltpu.VMEM((2, npp, 2 * tq, tq), BF16),
                        pltpu.VMEM((2, npp, 2 * tq, tq), BF16),
                        pltpu.SMEM((n_queue,), jnp.int32),
                        pltpu.SMEM((n_queue,), jnp.int32)],
        out_shape=jax.ShapeDtypeStruct((nb, seq, d), BF16),
        compiler_params=_params(("arbitrary", "arbitrary")),
        name="attn",
    )(q, k, v, tri, bias)


def kernel(x, c, ada_w, ada_b, mix_norm_g, mlp_norm_g, mlp_w1, mlp_w2, s5_a_re, s5_a_im, s5_log_dt,
           s5_b_re, s5_b_im, s5_c_re, s5_c_im, s5_d, s5_w_glu, kv_ada_w, kv_ada_b, kv_norm_g, w_kv,
           k_norm_g, sb_w_q, q_norm_g, sb_w_o):
    nb, seq, d = x.shape
    depth = ada_w.shape[0]
    n_a = s5_a_re.shape[0]
    assert d % S5_CH_PER_TILE == 0 and nb % 16 == 0 and seq % 256 == 0

    mods = _ada(c, ada_w, ada_b)
    kvmod = _ada(c, kv_ada_w[None], kv_ada_b[None])
    bf = lambda w: w.astype(BF16)

    q = k = v = None
    for i in range(depth):
        if i < n_a:
            g, p = s5_a_re.shape[1:]
            rows = g * p
            dt_rows = jnp.repeat(s5_log_dt[i], p).reshape(rows, 1)
            are, aim, bb_re, bb_im = _s5param(
                s5_a_re[i].reshape(rows, 1), s5_a_im[i].reshape(rows, 1), dt_rows,
                s5_b_re[i].reshape(rows, S5_GROUP), s5_b_im[i].reshape(rows, S5_GROUP))
            wb, wc = _s5_block_diag(bb_re.reshape(g, p, S5_GROUP), bb_im.reshape(g, p, S5_GROUP),
                                    s5_c_re[i], s5_c_im[i])
            gy = _s5(x, mix_norm_g[i], mods, i, wb, are.reshape(1, rows), aim.reshape(1, rows),
                     wc, s5_d[i])
            x = _post(gy, x, bf(s5_w_glu[i]), mods, i, mlp_norm_g[i], bf(mlp_w1[i]),
                      bf(mlp_w2[i]), glu=True)
        else:
            j = i - n_a
            if j == 0:
                k = v = None
            qn, kn, vn = _qkv(x, mix_norm_g[i], mods, i, kv_norm_g, kvmod,
                              bf(sb_w_q[j]), bf(w_kv), q_norm_g[j], k_norm_g)
            if k is None:
                k, v = kn, vn
            o = _attn(qn, k, v)
            x = _post(o, x, bf(sb_w_o[j]), mods, i, mlp_norm_g[i], bf(mlp_w1[i]),
                      bf(mlp_w2[i]), glu=False)
    return x
```
